```python
import math
import jax, jax.numpy as jnp
from jax import lax
import numpy as np

D_MODEL = 1024
BATCH = 16
SEQ = 4096
DEPTH = 1
DEC_BATCH = 8
DEC_SEQ = 64
PAST_LEN = 1024

CHUNK = 64
Q_BLOCK = 128
ROPE_THETA = 10000.0
NORM_EPS = 1e-6
NEG_INF = -1e30

MIX_WIDTH = D_MODEL
DIFF_HEADS = 4
DIFF_HEAD_DIM = 64
DIFF_WIDTH = DIFF_HEADS * 2 * DIFF_HEAD_DIM
MLA_HEADS = 4
MLA_Q_LORA = 256
MLA_KV_LORA = 128
MLA_NOPE = 128
MLA_ROPE = 64
MLA_V = 128
MLA_WIDTH = MLA_HEADS * MLA_V
MLA_SCALE = (MLA_NOPE + MLA_ROPE) ** -0.5
OFF_DQ = 0
OFF_DK = OFF_DQ + DIFF_WIDTH
OFF_DV = OFF_DK + DIFF_WIDTH
OFF_CQ = OFF_DV + DIFF_WIDTH
OFF_CKV = OFF_CQ + MLA_Q_LORA
OFF_KR = OFF_CKV + MLA_KV_LORA
IN_COLS = OFF_KR + MLA_ROPE
D_FF = 2816
CONV_W = 3

kernel_name = "hymba_diff_mla_convffn_stream_step"


def rmsnorm(x, g):
    xf = x.astype(jnp.float32)
    y = xf * lax.rsqrt(jnp.mean(xf * xf, axis=-1, keepdims=True) + NORM_EPS)
    return (y * g.astype(jnp.float32)).astype(x.dtype)


def rope(x, pos):
    dim = x.shape[-1]
    half = dim // 2
    inv = ROPE_THETA ** (-jnp.arange(half, dtype=jnp.float32) * (2.0 / dim))
    ang = pos.astype(jnp.float32)[:, None] * inv[None, :]
    shp = (ang.shape[0],) + (1,) * (x.ndim - 3) + (half,)
    cos = jnp.cos(ang).reshape(shp)
    sin = jnp.sin(ang).reshape(shp)
    xf = x.astype(jnp.float32)
    x1, x2 = xf[..., :half], xf[..., half:]
    return jnp.concatenate([x1 * cos - x2 * sin, x2 * cos + x1 * sin], axis=-1).astype(x.dtype)


def diff_attention(q, k, v, q_chunk, k_chunk, lam, lam_init, g_sub):
    s = jnp.einsum('bqhmd,bkhmd->bhmqk', q, k).astype(jnp.float32) * (DIFF_HEAD_DIM ** -0.5)
    vis = k_chunk[None, :] <= q_chunk[:, None]
    p = jax.nn.softmax(jnp.where(vis, s, NEG_INF), axis=-1)
    a = (p[:, :, 0] - lam * p[:, :, 1]).astype(v.dtype)
    o = jnp.einsum('bhqk,bkhe->bqhe', a, v)
    o = rmsnorm(o, g_sub) * (1.0 - lam_init)
    return o.reshape(o.shape[0], o.shape[1], DIFF_WIDTH)


def mla_attention(q_nope, q_rope, k_nope, k_rope, v, q_chunk, k_chunk):
    s = (jnp.einsum('bqhd,bkhd->bhqk', q_nope, k_nope)
         + jnp.einsum('bqhr,bkr->bhqk', q_rope, k_rope)).astype(jnp.float32) * MLA_SCALE
    vis = k_chunk[None, :] <= q_chunk[:, None]
    p = jax.nn.softmax(jnp.where(vis, s, NEG_INF), axis=-1)
    o = jnp.einsum('bhqk,bkhd->bqhd', p.astype(v.dtype), v)
    return o.reshape(o.shape[0], o.shape[1], MLA_WIDTH)


def over_query_blocks(fn, q_arrays, q_chunk):
    n_blk = q_chunk.shape[0] // Q_BLOCK

    def split(a):
        a = a.reshape((a.shape[0], n_blk, Q_BLOCK) + a.shape[2:])
        return jnp.moveaxis(a, 1, 0)

    xs = tuple(split(a) for a in q_arrays) + (q_chunk.reshape(n_blk, Q_BLOCK),)
    out = lax.map(lambda blk: fn(*blk), xs)
    out = jnp.moveaxis(out, 0, 1)
    return out.reshape(out.shape[0], n_blk * Q_BLOCK, out.shape[-1])


def hybrid_layer(x, pos, past, lam_init, g_attn, w_in, lq1, lk1, lq2, lk2, g_sub,
                 g_qa, w_qb, g_kva, w_kvb, w_out, g_ffn, w_up, w_conv, b_conv, w_down):
    B, T, _ = x.shape
    h = rmsnorm(x, g_attn)
    proj = h @ w_in
    q_d = rope(proj[..., OFF_DQ:OFF_DK].reshape(B, T, DIFF_HEADS, 2, DIFF_HEAD_DIM), pos)
    k_d = rope(proj[..., OFF_DK:OFF_DV].reshape(B, T, DIFF_HEADS, 2, DIFF_HEAD_DIM), pos)
    v_d = proj[..., OFF_DV:OFF_CQ].reshape(B, T, DIFF_HEADS, 2 * DIFF_HEAD_DIM)
    c_q = rmsnorm(proj[..., OFF_CQ:OFF_CKV], g_qa)
    c_kv = rmsnorm(proj[..., OFF_CKV:OFF_KR], g_kva)
    k_r = rope(proj[..., OFF_KR:IN_COLS], pos)
    q_m = (c_q @ w_qb).reshape(B, T, MLA_HEADS, MLA_NOPE + MLA_ROPE)
    q_nope = q_m[..., :MLA_NOPE]
    q_rope = rope(q_m[..., MLA_NOPE:], pos)

    if past is None:
        dk_all, dv_all, ckv_all, kr_all, k_pos = k_d, v_d, c_kv, k_r, pos
        conv_prev = jnp.zeros((B, CONV_W - 1, D_FF), x.dtype)
    else:
        p_dk, p_dv, p_ckv, p_kr, conv_prev = past
        P = p_dk.shape[1]
        dk_all = jnp.concatenate([p_dk, k_d], axis=1)
        dv_all = jnp.concatenate([p_dv, v_d], axis=1)
        ckv_all = jnp.concatenate([p_ckv, c_kv], axis=1)
        kr_all = jnp.concatenate([p_kr, k_r], axis=1)
        k_pos = jnp.concatenate([jnp.arange(P, dtype=jnp.int32), pos])
    Tk = ckv_all.shape[1]
    k_chunk = k_pos // CHUNK
    q_chunk = pos // CHUNK
    kv = (ckv_all @ w_kvb).reshape(B, Tk, MLA_HEADS, MLA_NOPE + MLA_V)
    k_nope, v_m = kv[..., :MLA_NOPE], kv[..., MLA_NOPE:]

    lam = (jnp.exp(jnp.sum(lq1.astype(jnp.float32) * lk1.astype(jnp.float32)))
           - jnp.exp(jnp.sum(lq2.astype(jnp.float32) * lk2.astype(jnp.float32))) + lam_init)

    def attend(qd, qn, qr, qc):
        od = diff_attention(qd, dk_all, dv_all, qc, k_chunk, lam, lam_init, g_sub)
        om = mla_attention(qn, qr, k_nope, kr_all, v_m, qc, k_chunk)
        return jnp.concatenate([od, om], axis=-1)

    if past is None:
        mix = over_query_blocks(attend, (q_d, q_nope, q_rope), q_chunk)
    else:
        mix = attend(q_d, q_nope, q_rope, q_chunk)
    x1 = x + mix @ w_out

    h2 = rmsnorm(x1, g_ffn)
    up = h2 @ w_up
    u, g = up[..., :D_FF], up[..., D_FF:]
    g_ext = jnp.concatenate([conv_prev, g], axis=1)
    conv = b_conv + sum(g_ext[:, j:j + T] * w_conv[j] for j in range(CONV_W))
    y = x1 + (jax.nn.silu(conv) * u) @ w_down
    new_conv = g_ext[:, -(CONV_W - 1):]
    return y, (k_d, v_d, c_kv, k_r, new_conv)


def setup_inputs(seed: int = 0) -> dict:
    key = jax.random.key(seed)
    ks = jax.random.split(key, 32)
    f32 = jnp.float32

    def nrm(k, shape, scale=1.0):
        return jax.random.normal(k, shape, f32) * scale

    def gain(k, shape):
        return 1.0 + 0.05 * jax.random.normal(k, shape, f32)

    return {
        "x_prompt": nrm(ks[0], (BATCH, SEQ, D_MODEL)),
        "x_sample": nrm(ks[1], (DEC_BATCH, DEC_SEQ, D_MODEL)),
        "cache_diff_k": nrm(ks[2], (DEPTH, DEC_BATCH, PAST_LEN, DIFF_HEADS, 2, DIFF_HEAD_DIM)),
        "cache_diff_v": nrm(ks[3], (DEPTH, DEC_BATCH, PAST_LEN, DIFF_HEADS, 2 * DIFF_HEAD_DIM)),
        "cache_mla_ckv": nrm(ks[4], (DEPTH, DEC_BATCH, PAST_LEN, MLA_KV_LORA)),
        "cache_mla_krope": nrm(ks[5], (DEPTH, DEC_BATCH, PAST_LEN, MLA_ROPE)),
        "state_conv": nrm(ks[6], (DEPTH, DEC_BATCH, CONV_W - 1, D_FF)),
        "g_attn": gain(ks[7], (DEPTH, D_MODEL)),
        "w_in": nrm(ks[8], (DEPTH, D_MODEL, IN_COLS), D_MODEL ** -0.5),
        "lambda_q1": nrm(ks[9], (DEPTH, DIFF_HEAD_DIM), 0.1),
        "lambda_k1": nrm(ks[10], (DEPTH, DIFF_HEAD_DIM), 0.1),
        "lambda_q2": nrm(ks[11], (DEPTH, DIFF_HEAD_DIM), 0.1),
        "lambda_k2": nrm(ks[12], (DEPTH, DIFF_HEAD_DIM), 0.1),
        "g_diff_sub": gain(ks[13], (DEPTH, 2 * DIFF_HEAD_DIM)),
        "g_q_lora": gain(ks[14], (DEPTH, MLA_Q_LORA)),
        "w_q_b": nrm(ks[15], (DEPTH, MLA_Q_LORA, MLA_HEADS * (MLA_NOPE + MLA_ROPE)), MLA_Q_LORA ** -0.5),
        "g_kv_lora": gain(ks[16], (DEPTH, MLA_KV_LORA)),
        "w_kv_b": nrm(ks[17], (DEPTH, MLA_KV_LORA, MLA_HEADS * (MLA_NOPE + MLA_V)), MLA_KV_LORA ** -0.5),
        "w_out": nrm(ks[18], (DEPTH, MIX_WIDTH, D_MODEL), MIX_WIDTH ** -0.5),
        "g_ffn": gain(ks[19], (DEPTH, D_MODEL)),
        "w_up": nrm(ks[20], (DEPTH, D_MODEL, 2 * D_FF), D_MODEL ** -0.5),
        "w_conv": nrm(ks[21], (DEPTH, CONV_W, D_FF), CONV_W ** -0.5),
        "b_conv": nrm(ks[22], (DEPTH, D_FF), 0.01),
        "w_down": nrm(ks[23], (DEPTH, D_FF, D_MODEL), D_FF ** -0.5),
        "g_final": gain(ks[24], (D_MODEL,)),
    }


def reference(x_prompt, x_sample, cache_diff_k, cache_diff_v, cache_mla_ckv, cache_mla_krope,
              state_conv, g_attn, w_in, lambda_q1, lambda_k1, lambda_q2, lambda_k2, g_diff_sub,
              g_q_lora, w_q_b, g_kv_lora, w_kv_b, w_out, g_ffn, w_up, w_conv, b_conv, w_down,
              g_final):
    S = x_prompt.shape[1]
    T = x_sample.shape[1]
    P = cache_diff_k.shape[2]
    pos_p = jnp.arange(S, dtype=jnp.int32)
    pos_s = P + jnp.arange(T, dtype=jnp.int32)
    hp, hs = x_prompt, x_sample
    st_p, st_s = [], []
    for l in range(DEPTH):
        lam_init = 0.8 - 0.6 * math.exp(-0.3 * l)
        wl = (g_attn[l], w_in[l], lambda_q1[l], lambda_k1[l], lambda_q2[l], lambda_k2[l],
              g_diff_sub[l], g_q_lora[l], w_q_b[l], g_kv_lora[l], w_kv_b[l], w_out[l],
              g_ffn[l], w_up[l], w_conv[l], b_conv[l], w_down[l])
        hp, sp = hybrid_layer(hp, pos_p, None, lam_init, *wl)
        past = (cache_diff_k[l], cache_diff_v[l], cache_mla_ckv[l], cache_mla_krope[l], state_conv[l])
        hs, ss = hybrid_layer(hs, pos_s, past, lam_init, *wl)
        st_p.append(sp)
        st_s.append(ss)
    y_prompt = rmsnorm(hp, g_final)
    y_sample = rmsnorm(hs, g_final)
    new_diff_k_p = jnp.stack([s[0] for s in st_p], 0)
    new_diff_v_p = jnp.stack([s[1] for s in st_p], 0)
    new_mla_ckv_p = jnp.stack([s[2] for s in st_p], 0)
    new_mla_krope_p = jnp.stack([s[3] for s in st_p], 0)
    new_conv_p = jnp.stack([s[4] for s in st_p], 0)
    new_diff_k_s = jnp.stack([s[0] for s in st_s], 0)
    new_diff_v_s = jnp.stack([s[1] for s in st_s], 0)
    new_mla_ckv_s = jnp.stack([s[2] for s in st_s], 0)
    new_mla_krope_s = jnp.stack([s[3] for s in st_s], 0)
    new_conv_s = jnp.stack([s[4] for s in st_s], 0)
    return (y_prompt, y_sample, new_diff_k_p, new_diff_v_p, new_mla_ckv_p, new_mla_krope_p,
            new_conv_p, new_diff_k_s, new_diff_v_s, new_mla_ckv_s, new_mla_krope_s, new_conv_s)
```

```python
import functools
import math

import jax
import jax.numpy as jnp
from jax import lax
from jax.experimental import pallas as pl
from jax.experimental.pallas import tpu as pltpu

D_MODEL = 1024
CHUNK = 64
ROPE_THETA = 10000.0
NORM_EPS = 1e-6
NEG_INF = -1e30

DIFF_HEADS = 4
DIFF_HEAD_DIM = 64
DIFF_WIDTH = DIFF_HEADS * 2 * DIFF_HEAD_DIM
MLA_HEADS = 4
MLA_Q_LORA = 256
MLA_KV_LORA = 128
MLA_NOPE = 128
MLA_ROPE = 64
MLA_V = 128
MLA_WIDTH = MLA_HEADS * MLA_V
MLA_SCALE = (MLA_NOPE + MLA_ROPE) ** -0.5
OFF_DQ = 0
OFF_DK = OFF_DQ + DIFF_WIDTH
OFF_DV = OFF_DK + DIFF_WIDTH
OFF_CQ = OFF_DV + DIFF_WIDTH
OFF_CKV = OFF_CQ + MLA_Q_LORA
OFF_KR = OFF_CKV + MLA_KV_LORA
IN_COLS = OFF_KR + MLA_ROPE
IN_COLS_PAD = IN_COLS + MLA_ROPE
D_FF = 2816
CONV_W = 3

LANES = 128
MLA_QK = 2 * LANES
FF_CHUNK = 256
LOG2E = math.log2(math.e)
VMEM_LIMIT = 56 * 1024 * 1024

_BF16 = jnp.bfloat16
_F32 = jnp.float32


def _dot(a, b):
    return jnp.dot(a, b, preferred_element_type=_F32)


def _dot_t(a, b):
    return lax.dot_general(a, b, (((1,), (1,)), ((), ())), preferred_element_type=_F32)


def _rms(xf, g):
    return xf * lax.rsqrt(jnp.mean(xf * xf, axis=-1, keepdims=True) + NORM_EPS) * g


def _rope_slab(xs, cos2, sin_lo, sin_hi):
    return (xs * cos2 + pltpu.roll(xs, LANES - 32, axis=1) * sin_lo
            + pltpu.roll(xs, 32, axis=1) * sin_hi)


def _proj_kernel(x_ref, cos_ref, slo_ref, shi_ref, g_attn_ref, w_in_ref, g_qa_ref, w_qn_ref,
                 w_qr_ref, wkt_ref, g_kva_ref,
                 qd_ref, kd32_ref, kd16_ref, vd32_ref, vd16_ref, ckv32_ref, kr32_ref,
                 kmla_ref, qmla_ref):
    x = x_ref[0]
    h = _rms(x, g_attn_ref[...]).astype(_BF16)
    proj = _dot(h, w_in_ref[...])
    cos2, slo, shi = cos_ref[...], slo_ref[...], shi_ref[...]
    rope = functools.partial(_rope_slab, cos2=cos2, sin_lo=slo, sin_hi=shi)

    q_scale = DIFF_HEAD_DIM ** -0.5 * LOG2E
    for s in range(DIFF_WIDTH // LANES):
        cs = slice(s * LANES, (s + 1) * LANES)
        qd_ref[0, :, cs] = (rope(proj[:, OFF_DQ + s * LANES:OFF_DQ + (s + 1) * LANES])
                            * q_scale).astype(_BF16)
        k = rope(proj[:, OFF_DK + s * LANES:OFF_DK + (s + 1) * LANES])
        kd32_ref[0, :, cs] = k
        kd16_ref[0, :, cs] = k.astype(_BF16)
    v = proj[:, OFF_DV:OFF_CQ]
    vd32_ref[0] = v
    vd16_ref[0] = v.astype(_BF16)

    c_q = _rms(proj[:, OFF_CQ:OFF_CKV], g_qa_ref[...]).astype(_BF16)
    c_kv = _rms(proj[:, OFF_CKV:OFF_KR], g_kva_ref[...])
    k_r2 = rope(proj[:, OFF_KR:IN_COLS_PAD])
    ckv32_ref[0] = c_kv
    kr32_ref[0] = k_r2[:, :MLA_ROPE]
    kmla_ref[0, :, :LANES] = c_kv.astype(_BF16)
    kmla_ref[0, :, LANES:] = k_r2.astype(_BF16)

    m_scale = MLA_SCALE * LOG2E
    q_nope = _dot(c_q, w_qn_ref[...]).astype(_BF16)
    q_rope = _dot(c_q, w_qr_ref[...])
    lane = lax.broadcasted_iota(jnp.int32, (1, LANES), 1)
    for hd in range(MLA_HEADS):
        q_lat = _dot(q_nope[:, hd * MLA_NOPE:(hd + 1) * MLA_NOPE], wkt_ref[hd])
        qmla_ref[0, :, hd * MLA_QK:hd * MLA_QK + LANES] = (q_lat * m_scale).astype(_BF16)
        if hd % 2 == 0:
            slab = rope(q_rope[:, (hd // 2) * LANES:(hd // 2 + 1) * LANES]) * m_scale
        own = (lane < MLA_ROPE) if hd % 2 == 0 else (lane >= MLA_ROPE)
        qmla_ref[0, :, hd * MLA_QK + LANES:(hd + 1) * MLA_QK] = (
            jnp.where(own, slab, 0.0).astype(_BF16))


def _proj_call(x, tabs, g_attn, w_in_p, g_qa, w_qn, w_qr, wkt, g_kva, tm):
    B, T, _ = x.shape
    nt = T // tm
    row = lambda c: pl.BlockSpec((1, tm, c), lambda b, t: (b, t, 0))
    tab = pl.BlockSpec((tm, LANES), lambda b, t: (t, 0))
    full = lambda a: pl.BlockSpec(a.shape, lambda b, t: (0,) * a.ndim)
    out_shapes = (
        jax.ShapeDtypeStruct((B, T, DIFF_WIDTH), _BF16),
        jax.ShapeDtypeStruct((B, T, DIFF_WIDTH), _F32),
        jax.ShapeDtypeStruct((B, T, DIFF_WIDTH), _BF16),
        jax.ShapeDtypeStruct((B, T, DIFF_WIDTH), _F32),
        jax.ShapeDtypeStruct((B, T, DIFF_WIDTH), _BF16),
        jax.ShapeDtypeStruct((B, T, MLA_KV_LORA), _F32),
        jax.ShapeDtypeStruct((B, T, MLA_ROPE), _F32),
        jax.ShapeDtypeStruct((B, T, MLA_QK), _BF16),
        jax.ShapeDtypeStruct((B, T, MLA_HEADS * MLA_QK), _BF16),
    )
    out_specs = (row(DIFF_WIDTH), row(DIFF_WIDTH), row(DIFF_WIDTH), row(DIFF_WIDTH),
                 row(DIFF_WIDTH), row(MLA_KV_LORA), row(MLA_ROPE), row(MLA_QK),
                 row(MLA_HEADS * MLA_QK))
    return pl.pallas_call(
        _proj_kernel,
        grid=(B, nt),
        in_specs=[row(D_MODEL), tab, tab, tab, full(g_attn), full(w_in_p), full(g_qa),
                  full(w_qn), full(w_qr), full(wkt), full(g_kva)],
        out_specs=out_specs,
        out_shape=out_shapes,
        compiler_params=pltpu.CompilerParams(
            dimension_semantics=("arbitrary", "arbitrary"), vmem_limit_bytes=VMEM_LIMIT),
        name="proj",
    )(x, *tabs, g_attn, w_in_p, g_qa, w_qn, w_qr, wkt, g_kva)


def _flash(qs, k_ref, v_of, m_ref, l_ref, acc_ref, *, qi, n_stack, tq, tk, q_pos0, n_keys):
    m_ref[...] = jnp.full(m_ref.shape, NEG_INF, _F32)
    l_ref[...] = jnp.zeros(l_ref.shape, _F32)
    acc_ref[...] = jnp.zeros(acc_ref.shape, _F32)

    first_pos = q_pos0 + qi * tq
    end_first = jnp.minimum((first_pos // CHUNK + 1) * CHUNK, n_keys)
    end_last = jnp.minimum(((first_pos + tq - 1) // CHUNK + 1) * CHUNK, n_keys)
    n_full = end_first // tk
    n_vis = (end_last + tk - 1) // tk

    def step(j, masked):
        k0 = pl.multiple_of(j * tk, tk)
        kt = k_ref[0, pl.ds(k0, tk), :]
        s = _dot_t(qs, kt)
        if masked:
            rows = lax.broadcasted_iota(jnp.int32, (tq, tk), 0) + first_pos
            cols = lax.broadcasted_iota(jnp.int32, (tq, tk), 1) + k0
            vis = ((cols // CHUNK) <= (rows // CHUNK)) & (cols < n_keys)
            s = jnp.where(vis[None], s.reshape(n_stack, tq, tk), NEG_INF)
            s = s.reshape(n_stack * tq, tk)
        m_prev = m_ref[...]
        m_new = jnp.maximum(m_prev, jnp.max(s, axis=1, keepdims=True))
        alpha = jnp.exp2(m_prev - m_new)
        p = jnp.exp2(s - m_new)
        l_ref[...] = alpha * l_ref[...] + jnp.sum(p, axis=1, keepdims=True)
        acc_ref[...] = alpha * acc_ref[...] + _dot(p.astype(_BF16), v_of(k0))
        m_ref[...] = m_new

    lax.fori_loop(0, n_full, lambda j, c: (step(j, False), c)[1], 0)
    lax.fori_loop(n_full, n_vis, lambda j, c: (step(j, True), c)[1], 0)


def _diff_attn_kernel(q_ref, k_ref, v_ref, lam_ref, gsub_ref, o_ref, m_ref, l_ref, acc_ref,
                      *, tq, tk, q_pos0, n_keys, lam_init):
    qi = pl.program_id(2)
    q = q_ref[0]
    lane = lax.broadcasted_iota(jnp.int32, (1, LANES), 1)
    zero = jnp.zeros_like(q)
    qs = jnp.concatenate([jnp.where(lane < DIFF_HEAD_DIM, q, zero),
                          jnp.where(lane >= DIFF_HEAD_DIM, q, zero)], axis=0)
    _flash(qs, k_ref, lambda k0: v_ref[0, pl.ds(k0, tk), :], m_ref, l_ref, acc_ref,
           qi=qi, n_stack=2, tq=tq, tk=tk, q_pos0=q_pos0, n_keys=n_keys)
    lp = lam_ref[...]
    lam = (jnp.exp(jnp.sum(lp[0:1] * lp[1:2], axis=1, keepdims=True))
           - jnp.exp(jnp.sum(lp[2:3] * lp[3:4], axis=1, keepdims=True)) + lam_init)
    o = acc_ref[...] / l_ref[...]
    o = o[:tq] - lam * o[tq:]
    o_ref[0] = (_rms(o, gsub_ref[...]) * (1.0 - lam_init)).astype(o_ref.dtype)


def _diff_attn_call(qd, kd, vd, lam_p, g_sub, *, tq, tk, q_pos0, n_keys, lam_init):
    B, T, _ = qd.shape
    Sk = kd.shape[1]
    kern = functools.partial(_diff_attn_kernel, tq=tq, tk=tk, q_pos0=q_pos0, n_keys=n_keys,
                             lam_init=lam_init)
    kv = pl.BlockSpec((1, Sk, LANES), lambda b, h, i: (b, 0, h))
    qo = pl.BlockSpec((1, tq, LANES), lambda b, h, i: (b, i, h))
    full = lambda a: pl.BlockSpec(a.shape, lambda b, h, i: (0,) * a.ndim)
    R = 2 * tq
    return pl.pallas_call(
        kern,
        grid=(B, DIFF_HEADS, T // tq),
        in_specs=[qo, kv, kv, full(lam_p), full(g_sub)],
        out_specs=qo,
        out_shape=jax.ShapeDtypeStruct((B, T, DIFF_WIDTH), _BF16),
        scratch_shapes=[pltpu.VMEM((R, 1), _F32), pltpu.VMEM((R, 1), _F32),
                        pltpu.VMEM((R, LANES), _F32)],
        compiler_params=pltpu.CompilerParams(
            dimension_semantics=("arbitrary",) * 3, vmem_limit_bytes=VMEM_LIMIT),
        name="diff_attn",
    )(qd, kd, vd, lam_p, g_sub)


def _mla_attn_kernel(q_ref, k_ref, wv_ref, o_ref, m_ref, l_ref, acc_ref,
                     *, tq, tk, q_pos0, n_keys):
    qi = pl.program_id(1)
    q = q_ref[0]
    qs = jnp.concatenate([q[:, h * MLA_QK:(h + 1) * MLA_QK] for h in range(MLA_HEADS)], axis=0)
    _flash(qs, k_ref, lambda k0: k_ref[0, pl.ds(k0, tk), :MLA_KV_LORA], m_ref, l_ref, acc_ref,
           qi=qi, n_stack=MLA_HEADS, tq=tq, tk=tk, q_pos0=q_pos0, n_keys=n_keys)
    o_lat = (acc_ref[...] / l_ref[...]).astype(_BF16)
    for h in range(MLA_HEADS):
        o_ref[0, :, h * MLA_V:(h + 1) * MLA_V] = _dot(
            o_lat[h * tq:(h + 1) * tq], wv_ref[h]).astype(o_ref.dtype)


def _mla_attn_call(qm, kk, wv, *, tq, tk, q_pos0, n_keys):
    B, T, _ = qm.shape
    Sk = kk.shape[1]
    kern = functools.partial(_mla_attn_kernel, tq=tq, tk=tk, q_pos0=q_pos0, n_keys=n_keys)
    R = MLA_HEADS * tq
    return pl.pallas_call(
        kern,
        grid=(B, T // tq),
        in_specs=[pl.BlockSpec((1, tq, MLA_HEADS * MLA_QK), lambda b, i: (b, i, 0)),
                  pl.BlockSpec((1, Sk, MLA_QK), lambda b, i: (b, 0, 0)),
                  pl.BlockSpec(wv.shape, lambda b, i: (0, 0, 0))],
        out_specs=pl.BlockSpec((1, tq, MLA_WIDTH), lambda b, i: (b, i, 0)),
        out_shape=jax.ShapeDtypeStruct((B, T, MLA_WIDTH), _BF16),
        scratch_shapes=[pltpu.VMEM((R, 1), _F32), pltpu.VMEM((R, 1), _F32),
                        pltpu.VMEM((R, MLA_KV_LORA), _F32)],
        compiler_params=pltpu.CompilerParams(
            dimension_semantics=("arbitrary",) * 2, vmem_limit_bytes=VMEM_LIMIT),
        name="mla_attn",
    )(qm, kk, wv)


def _ffn_kernel(x_ref, md_ref, mm_ref, cprev_ref, w_od_ref, w_om_ref, g_ffn_ref, w_up_ref,
                w_conv_ref, b_conv_ref, w_down_ref, g_fin_ref,
                y_ref, nconv_ref, carry_ref, gbuf_ref, act_ref, *, tm):
    t = pl.program_id(1)

    @pl.when(t == 0)
    def _():
        carry_ref[...] = jnp.zeros(carry_ref.shape, _F32)
        carry_ref[8 - (CONV_W - 1):, :] = cprev_ref[0]

    x1 = x_ref[0] + _dot(md_ref[0], w_od_ref[...]) + _dot(mm_ref[0], w_om_ref[...])
    h2 = _rms(x1, g_ffn_ref[...]).astype(_BF16)
    for c in range(D_FF // FF_CHUNK):
        cs = slice(c * FF_CHUNK, (c + 1) * FF_CHUNK)
        u = _dot(h2, w_up_ref[:, cs])
        g = _dot(h2, w_up_ref[:, D_FF + c * FF_CHUNK:D_FF + (c + 1) * FF_CHUNK])
        gbuf_ref[0:8, :] = carry_ref[:, cs]
        gbuf_ref[8:, :] = g
        conv = (b_conv_ref[:, cs] + gbuf_ref[6:6 + tm, :] * w_conv_ref[0:1, cs]
                + gbuf_ref[7:7 + tm, :] * w_conv_ref[1:2, cs] + g * w_conv_ref[2:3, cs])
        act_ref[:, cs] = (conv * jax.nn.sigmoid(conv) * u).astype(_BF16)
        carry_ref[:, cs] = g[tm - 8:, :]
    nconv_ref[0] = carry_ref[8 - (CONV_W - 1):, :]
    y_ref[0] = _rms(x1 + _dot(act_ref[...], w_down_ref[...]), g_fin_ref[...])


def _ffn_call(x, mix_d, mix_m, conv_prev, w_od, w_om, g_ffn, w_up, w_conv, b_conv, w_down,
              g_fin, tm):
    B, T, _ = x.shape
    row = lambda c: pl.BlockSpec((1, tm, c), lambda b, t: (b, t, 0))
    full = lambda a: pl.BlockSpec(a.shape, lambda b, t: (0,) * a.ndim,
                                  pipeline_mode=pl.Buffered(1))
    state = pl.BlockSpec((1, CONV_W - 1, D_FF), lambda b, t: (b, 0, 0))
    return pl.pallas_call(
        functools.partial(_ffn_kernel, tm=tm),
        grid=(B, T // tm),
        in_specs=[row(D_MODEL), row(DIFF_WIDTH), row(MLA_WIDTH), state, full(w_od), full(w_om),
                  full(g_ffn), full(w_up), full(w_conv), full(b_conv), full(w_down),
                  full(g_fin)],
        out_specs=(row(D_MODEL), state),
        out_shape=(jax.ShapeDtypeStruct((B, T, D_MODEL), _F32),
                   jax.ShapeDtypeStruct((B, CONV_W - 1, D_FF), _F32)),
        scratch_shapes=[pltpu.VMEM((8, D_FF), _F32), pltpu.VMEM((tm + 8, FF_CHUNK), _F32),
                        pltpu.VMEM((tm, D_FF), _BF16)],
        compiler_params=pltpu.CompilerParams(
            dimension_semantics=("arbitrary", "arbitrary"), vmem_limit_bytes=VMEM_LIMIT),
        name="ffn",
    )(x, mix_d, mix_m, conv_prev, w_od, w_om, g_ffn, w_up, w_conv, b_conv, w_down, g_fin)


def _rope_tables(pos):
    half = DIFF_HEAD_DIM // 2
    inv = ROPE_THETA ** (-jnp.arange(half, dtype=_F32) * (2.0 / DIFF_HEAD_DIM))
    ang = pos.astype(_F32)[:, None] * inv[None, :]
    cos, sin, zero = jnp.cos(ang), jnp.sin(ang), jnp.zeros_like(ang)
    cos2 = jnp.concatenate([cos, cos, cos, cos], axis=1)
    sin_lo = jnp.concatenate([-sin, zero, -sin, zero], axis=1)
    sin_hi = jnp.concatenate([zero, sin, zero, sin], axis=1)
    return cos2, sin_lo, sin_hi


def _layer(x, pos, past, lam_init, wl, g_final, *, tm, tq_d, tq_m):
    (g_attn, w_in_p, lam_p, g_sub, g_qa, w_qn, w_qr, wkt, wv, g_kva, w_od, w_om, g_ffn, w_up,
     w_conv, b_conv, w_down) = wl
    B, T, _ = x.shape
    tabs = _rope_tables(pos)
    qd, kd32, kd16, vd32, vd16, ckv32, kr32, kmla, qmla = _proj_call(
        x, tabs, g_attn, w_in_p, g_qa, w_qn, w_qr, wkt, g_kva, tm)

    if past is None:
        kd_all, vd_all, kmla_all = kd16, vd16, kmla
        conv_prev = jnp.zeros((B, CONV_W - 1, D_FF), _F32)
        n_keys, q_pos0, tk_d, tk_m = T, 0, tq_d, tq_m
    else:
        p_dk, p_dv, p_ckv, p_kr, conv_prev = past
        P = p_dk.shape[1]
        n_keys, q_pos0 = P + T, P
        sk = -(-n_keys // LANES) * LANES
        tk_d = tk_m = sk

        def cat(old, new):
            a = jnp.concatenate([old.astype(_BF16), new], axis=1)
            return jnp.pad(a, ((0, 0), (0, sk - n_keys), (0, 0)))

        kd_all = cat(p_dk.reshape(B, P, DIFF_WIDTH), kd16)
        vd_all = cat(p_dv.reshape(B, P, DIFF_WIDTH), vd16)
        kmla_all = cat(jnp.concatenate([p_ckv, p_kr, p_kr], axis=-1), kmla)

    mix_d = _diff_attn_call(qd, kd_all, vd_all, lam_p, g_sub, tq=tq_d, tk=tk_d, q_pos0=q_pos0,
                            n_keys=n_keys, lam_init=lam_init)
    mix_m = _mla_attn_call(qmla, kmla_all, wv, tq=tq_m, tk=tk_m, q_pos0=q_pos0, n_keys=n_keys)
    y, new_conv = _ffn_call(x, mix_d, mix_m, conv_prev, w_od, w_om, g_ffn, w_up, w_conv, b_conv,
                            w_down, g_final, tm)
    state = (kd32.reshape(1, B, T, DIFF_HEADS, 2, DIFF_HEAD_DIM),
             vd32.reshape(1, B, T, DIFF_HEADS, 2 * DIFF_HEAD_DIM),
             ckv32[None], kr32[None], new_conv[None])
    return y, state


def kernel(x_prompt, x_sample, cache_diff_k, cache_diff_v, cache_mla_ckv, cache_mla_krope,
           state_conv, g_attn, w_in, lambda_q1, lambda_k1, lambda_q2, lambda_k2, g_diff_sub,
           g_q_lora, w_q_b, g_kv_lora, w_kv_b, w_out, g_ffn, w_up, w_conv, b_conv, w_down,
           g_final):
    assert g_attn.shape[0] == 1, "single-layer model"
    S = x_prompt.shape[1]
    T = x_sample.shape[1]
    P = cache_diff_k.shape[2]
    lam_init = 0.8 - 0.6 * math.exp(-0.3 * 0)

    w_in_p = jnp.concatenate([w_in[0], w_in[0][:, OFF_KR:IN_COLS]], axis=1).astype(_BF16)
    wq = w_q_b[0].reshape(MLA_Q_LORA, MLA_HEADS, MLA_NOPE + MLA_ROPE)
    w_qn = wq[:, :, :MLA_NOPE].reshape(MLA_Q_LORA, MLA_HEADS * MLA_NOPE).astype(_BF16)
    w_qr = wq[:, :, MLA_NOPE:].reshape(MLA_Q_LORA, MLA_HEADS * MLA_ROPE).astype(_BF16)
    wkv = w_kv_b[0].reshape(MLA_KV_LORA, MLA_HEADS, MLA_NOPE + MLA_V)
    wkt = jnp.transpose(wkv[:, :, :MLA_NOPE], (1, 2, 0)).astype(_BF16)
    wv = jnp.transpose(wkv[:, :, MLA_NOPE:], (1, 0, 2)).astype(_BF16)
    lam_p = jnp.concatenate([lambda_q1, lambda_k1, lambda_q2, lambda_k2], axis=0)
    wl = (g_attn, w_in_p, lam_p, g_diff_sub, g_q_lora, w_qn, w_qr, wkt, wv, g_kv_lora,
          w_out[0][:DIFF_WIDTH].astype(_BF16), w_out[0][DIFF_WIDTH:].astype(_BF16), g_ffn,
          w_up[0].astype(_BF16), w_conv[0], b_conv, w_down[0].astype(_BF16))
    g_fin = g_final[None]

    pos_p = jnp.arange(S, dtype=jnp.int32)
    pos_s = P + jnp.arange(T, dtype=jnp.int32)
    y_p, st_p = _layer(x_prompt, pos_p, None, lam_init, wl, g_fin, tm=512, tq_d=512, tq_m=256)
    past = (cache_diff_k[0], cache_diff_v[0], cache_mla_ckv[0], cache_mla_krope[0], state_conv[0])
    y_s, st_s = _layer(x_sample, pos_s, past, lam_init, wl, g_fin, tm=T, tq_d=T, tq_m=T)
    return (y_p, y_s) + st_p + st_s
```

```python
import functools
import math

import jax
import jax.numpy as jnp
from jax import lax
from jax.experimental import pallas as pl
from jax.experimental.pallas import tpu as pltpu

D_MODEL = 1024
CHUNK = 64
ROPE_THETA = 10000.0
NORM_EPS = 1e-6
NEG_INF = -1e30

DIFF_HEADS = 4
DIFF_HEAD_DIM = 64
DIFF_WIDTH = DIFF_HEADS * 2 * DIFF_HEAD_DIM
MLA_HEADS = 4
MLA_Q_LORA = 256
MLA_KV_LORA = 128
MLA_NOPE = 128
MLA_ROPE = 64
MLA_V = 128
MLA_WIDTH = MLA_HEADS * MLA_V
MLA_SCALE = (MLA_NOPE + MLA_ROPE) ** -0.5
OFF_DQ = 0
OFF_DK = OFF_DQ + DIFF_WIDTH
OFF_DV = OFF_DK + DIFF_WIDTH
OFF_CQ = OFF_DV + DIFF_WIDTH
OFF_CKV = OFF_CQ + MLA_Q_LORA
OFF_KR = OFF_CKV + MLA_KV_LORA
IN_COLS = OFF_KR + MLA_ROPE
IN_COLS_PAD = IN_COLS + MLA_ROPE
D_FF = 2816
CONV_W = 3

LANES = 128
MLA_QK = 2 * LANES
V_ONES = 2 * LANES
FF_CHUNK = 256
LOG2E = math.log2(math.e)
VMEM_LIMIT = 56 * 1024 * 1024

_BF16 = jnp.bfloat16
_F32 = jnp.float32


def _dot(a, b):
    return jnp.dot(a, b, preferred_element_type=_F32)


def _dot_t(a, b):
    return lax.dot_general(a, b, (((1,), (1,)), ((), ())), preferred_element_type=_F32)


def _rms(xf, g):
    return xf * lax.rsqrt(jnp.mean(xf * xf, axis=-1, keepdims=True) + NORM_EPS) * g


def _rope_slab(xs, cos2, sin_lo, sin_hi):
    return (xs * cos2 + pltpu.roll(xs, LANES - 32, axis=1) * sin_lo
            + pltpu.roll(xs, 32, axis=1) * sin_hi)


def _proj_kernel(x_ref, cos_ref, slo_ref, shi_ref, g_attn_ref, w_in_ref, g_qa_ref, w_qn_ref,
                 w_qr_ref, wkt_ref, g_kva_ref,
                 qd_ref, kd32_ref, kd16_ref, vd32_ref, vd16_ref, ckv32_ref, kr32_ref,
                 kmla_ref, vmla_ref, qmla_ref):
    x = x_ref[0]
    h = _rms(x, g_attn_ref[...]).astype(_BF16)
    proj = _dot(h, w_in_ref[...])
    cos2, slo, shi = cos_ref[...], slo_ref[...], shi_ref[...]
    rope = functools.partial(_rope_slab, cos2=cos2, sin_lo=slo, sin_hi=shi)
    ones = jnp.ones((x.shape[0], LANES), _BF16)

    q_scale = DIFF_HEAD_DIM ** -0.5 * LOG2E
    for s in range(DIFF_HEADS):
        cs = slice(s * LANES, (s + 1) * LANES)
        qd_ref[0, :, cs] = (rope(proj[:, OFF_DQ + s * LANES:OFF_DQ + (s + 1) * LANES])
                            * q_scale).astype(_BF16)
        k = rope(proj[:, OFF_DK + s * LANES:OFF_DK + (s + 1) * LANES])
        kd32_ref[0, :, cs] = k
        kd16_ref[0, :, cs] = k.astype(_BF16)
        v = proj[:, OFF_DV + s * LANES:OFF_DV + (s + 1) * LANES]
        vd32_ref[0, :, cs] = v
        vd16_ref[0, :, s * V_ONES:s * V_ONES + LANES] = v.astype(_BF16)
        vd16_ref[0, :, s * V_ONES + LANES:(s + 1) * V_ONES] = ones

    c_q = _rms(proj[:, OFF_CQ:OFF_CKV], g_qa_ref[...]).astype(_BF16)
    c_kv = _rms(proj[:, OFF_CKV:OFF_KR], g_kva_ref[...])
    k_r2 = rope(proj[:, OFF_KR:IN_COLS_PAD])
    ckv32_ref[0] = c_kv
    kr32_ref[0] = k_r2[:, :MLA_ROPE]
    kmla_ref[0, :, :LANES] = c_kv.astype(_BF16)
    kmla_ref[0, :, LANES:] = k_r2.astype(_BF16)
    vmla_ref[0, :, :LANES] = c_kv.astype(_BF16)
    vmla_ref[0, :, LANES:] = ones

    m_scale = MLA_SCALE * LOG2E
    q_nope = _dot(c_q, w_qn_ref[...]).astype(_BF16)
    q_rope = _dot(c_q, w_qr_ref[...])
    lane = lax.broadcasted_iota(jnp.int32, (1, LANES), 1)
    for hd in range(MLA_HEADS):
        q_lat = _dot(q_nope[:, hd * MLA_NOPE:(hd + 1) * MLA_NOPE], wkt_ref[hd])
        qmla_ref[0, :, hd * MLA_QK:hd * MLA_QK + LANES] = (q_lat * m_scale).astype(_BF16)
        if hd % 2 == 0:
            slab = rope(q_rope[:, (hd // 2) * LANES:(hd // 2 + 1) * LANES]) * m_scale
        own = (lane < MLA_ROPE) if hd % 2 == 0 else (lane >= MLA_ROPE)
        qmla_ref[0, :, hd * MLA_QK + LANES:(hd + 1) * MLA_QK] = (
            jnp.where(own, slab, 0.0).astype(_BF16))


def _proj_call(x, tabs, g_attn, w_in_p, g_qa, w_qn, w_qr, wkt, g_kva, tm):
    B, T, _ = x.shape
    nt = T // tm
    row = lambda c: pl.BlockSpec((1, tm, c), lambda b, t: (b, t, 0))
    tab = pl.BlockSpec((tm, LANES), lambda b, t: (t, 0))
    full = lambda a: pl.BlockSpec(a.shape, lambda b, t: (0,) * a.ndim)
    outs = (
        (DIFF_WIDTH, _BF16),
        (DIFF_WIDTH, _F32),
        (DIFF_WIDTH, _BF16),
        (DIFF_WIDTH, _F32),
        (DIFF_HEADS * V_ONES, _BF16),
        (MLA_KV_LORA, _F32),
        (MLA_ROPE, _F32),
        (MLA_QK, _BF16),
        (V_ONES, _BF16),
        (MLA_HEADS * MLA_QK, _BF16),
    )
    return pl.pallas_call(
        _proj_kernel,
        grid=(B, nt),
        in_specs=[row(D_MODEL), tab, tab, tab, full(g_attn), full(w_in_p), full(g_qa),
                  full(w_qn), full(w_qr), full(wkt), full(g_kva)],
        out_specs=tuple(row(c) for c, _ in outs),
        out_shape=tuple(jax.ShapeDtypeStruct((B, T, c), d) for c, d in outs),
        compiler_params=pltpu.CompilerParams(
            dimension_semantics=("arbitrary", "arbitrary"), vmem_limit_bytes=VMEM_LIMIT),
        name="proj",
    )(x, *tabs, g_attn, w_in_p, g_qa, w_qn, w_qr, wkt, g_kva)


def _flash(qs, k_ref, v_ref, s_ref, m_ref, a_ref, acc_ref, *, qi, n_stack, tq, tk, q_pos0,
           n_keys):
    rows = n_stack * tq
    first_pos = q_pos0 + qi * tq
    end_first = jnp.minimum((first_pos // CHUNK + 1) * CHUNK, n_keys)
    n_full = end_first // tk

    def score(j, masked):
        k0 = pl.multiple_of(j * tk, tk)
        s = _dot_t(qs, k_ref[0, pl.ds(k0, tk), :])
        if masked:
            r = lax.broadcasted_iota(jnp.int32, (tq, tk), 0) + first_pos
            c = lax.broadcasted_iota(jnp.int32, (tq, tk), 1) + k0
            vis = ((c // CHUNK) <= (r // CHUNK)) & (c < n_keys)
            s = jnp.where(vis[None], s.reshape(n_stack, tq, tk), NEG_INF).reshape(rows, tk)
        s_ref[...] = s
        m_prev = m_ref[...]
        m_new = jnp.maximum(m_prev, jnp.max(s, axis=1, keepdims=True))
        a_ref[...] = jnp.exp2(m_prev - m_new)
        m_ref[...] = m_new

    def finish(j):
        k0 = pl.multiple_of(j * tk, tk)
        p = jnp.exp2(s_ref[...] - jnp.tile(m_ref[...], (1, tk // LANES)))
        pv = _dot(p.astype(_BF16), v_ref[0, pl.ds(k0, tk), :])
        acc_ref[...] = jnp.tile(a_ref[...], (1, V_ONES // LANES)) * acc_ref[...] + pv

    m_ref[...] = jnp.full(m_ref.shape, NEG_INF, _F32)
    acc_ref[...] = jnp.zeros(acc_ref.shape, _F32)
    score(n_full, True)

    def trip(j, prev):
        finish(prev)
        score(j, False)
        return j

    last = lax.fori_loop(0, n_full, trip, n_full)
    finish(last)
    acc = acc_ref[...]
    return acc[:, :LANES] / acc[:, LANES:]


def _attn_scratch(rows, tk):
    return [pltpu.VMEM((rows, tk), _F32), pltpu.VMEM((rows, LANES), _F32),
            pltpu.VMEM((rows, LANES), _F32), pltpu.VMEM((rows, V_ONES), _F32)]


def _check_tiles(T, Sk, tq, tk, q_pos0, n_keys):
    assert T % tq == 0 and Sk % tk == 0 and tk % LANES == 0
    for qi in range(T // tq):
        first = q_pos0 + qi * tq
        n_full = min((first // CHUNK + 1) * CHUNK, n_keys) // tk
        n_vis = -(-min(((first + tq - 1) // CHUNK + 1) * CHUNK, n_keys) // tk)
        assert n_vis == n_full + 1 and n_vis * tk <= Sk, (qi, n_full, n_vis)


def _diff_attn_kernel(q_ref, k_ref, v_ref, lam_ref, gsub_ref, o_ref, s_ref, m_ref, a_ref,
                      acc_ref, *, tq, tk, q_pos0, n_keys, lam_init):
    qi = pl.program_id(2)
    q = q_ref[0]
    lane = lax.broadcasted_iota(jnp.int32, (1, LANES), 1)
    zero = jnp.zeros_like(q)
    qs = jnp.concatenate([jnp.where(lane < DIFF_HEAD_DIM, q, zero),
                          jnp.where(lane >= DIFF_HEAD_DIM, q, zero)], axis=0)
    o = _flash(qs, k_ref, v_ref, s_ref, m_ref, a_ref, acc_ref, qi=qi, n_stack=2, tq=tq, tk=tk,
               q_pos0=q_pos0, n_keys=n_keys)
    lp = lam_ref[...]
    lam = (jnp.exp(jnp.sum(lp[0:1] * lp[1:2], axis=1, keepdims=True))
           - jnp.exp(jnp.sum(lp[2:3] * lp[3:4], axis=1, keepdims=True)) + lam_init)
    o = o[:tq] - lam * o[tq:]
    o_ref[0] = (_rms(o, gsub_ref[...]) * (1.0 - lam_init)).astype(o_ref.dtype)


def _diff_attn_call(qd, kd, vd, lam_p, g_sub, *, tq, tk, q_pos0, n_keys, lam_init):
    B, T, _ = qd.shape
    Sk = kd.shape[1]
    _check_tiles(T, Sk, tq, tk, q_pos0, n_keys)
    kern = functools.partial(_diff_attn_kernel, tq=tq, tk=tk, q_pos0=q_pos0, n_keys=n_keys,
                             lam_init=lam_init)
    kspec = pl.BlockSpec((1, Sk, LANES), lambda b, h, i: (b, 0, h))
    vspec = pl.BlockSpec((1, Sk, V_ONES), lambda b, h, i: (b, 0, h))
    qo = pl.BlockSpec((1, tq, LANES), lambda b, h, i: (b, i, h))
    full = lambda a: pl.BlockSpec(a.shape, lambda b, h, i: (0,) * a.ndim)
    return pl.pallas_call(
        kern,
        grid=(B, DIFF_HEADS, T // tq),
        in_specs=[qo, kspec, vspec, full(lam_p), full(g_sub)],
        out_specs=qo,
        out_shape=jax.ShapeDtypeStruct((B, T, DIFF_WIDTH), _BF16),
        scratch_shapes=_attn_scratch(2 * tq, tk),
        compiler_params=pltpu.CompilerParams(
            dimension_semantics=("arbitrary",) * 3, vmem_limit_bytes=VMEM_LIMIT),
        name="diff_attn",
    )(qd, kd, vd, lam_p, g_sub)


def _mla_attn_kernel(q_ref, k_ref, v_ref, wv_ref, o_ref, s_ref, m_ref, a_ref, acc_ref,
                     *, tq, tk, q_pos0, n_keys):
    qi = pl.program_id(1)
    q = q_ref[0]
    qs = jnp.concatenate([q[:, h * MLA_QK:(h + 1) * MLA_QK] for h in range(MLA_HEADS)], axis=0)
    o_lat = _flash(qs, k_ref, v_ref, s_ref, m_ref, a_ref, acc_ref, qi=qi, n_stack=MLA_HEADS,
                   tq=tq, tk=tk, q_pos0=q_pos0, n_keys=n_keys).astype(_BF16)
    for h in range(MLA_HEADS):
        o_ref[0, :, h * MLA_V:(h + 1) * MLA_V] = _dot(
            o_lat[h * tq:(h + 1) * tq], wv_ref[h]).astype(o_ref.dtype)


def _mla_attn_call(qm, kk, vv, wv, *, tq, tk, q_pos0, n_keys):
    B, T, _ = qm.shape
    Sk = kk.shape[1]
    _check_tiles(T, Sk, tq, tk, q_pos0, n_keys)
    kern = functools.partial(_mla_attn_kernel, tq=tq, tk=tk, q_pos0=q_pos0, n_keys=n_keys)
    kv = pl.BlockSpec((1, Sk, MLA_QK), lambda b, i: (b, 0, 0))
    return pl.pallas_call(
        kern,
        grid=(B, T // tq),
        in_specs=[pl.BlockSpec((1, tq, MLA_HEADS * MLA_QK), lambda b, i: (b, i, 0)), kv, kv,
                  pl.BlockSpec(wv.shape, lambda b, i: (0, 0, 0))],
        out_specs=pl.BlockSpec((1, tq, MLA_WIDTH), lambda b, i: (b, i, 0)),
        out_shape=jax.ShapeDtypeStruct((B, T, MLA_WIDTH), _BF16),
        scratch_shapes=_attn_scratch(MLA_HEADS * tq, tk),
        compiler_params=pltpu.CompilerParams(
            dimension_semantics=("arbitrary",) * 2, vmem_limit_bytes=VMEM_LIMIT),
        name="mla_attn",
    )(qm, kk, vv, wv)


def _ffn_kernel(x_ref, md_ref, mm_ref, cprev_ref, w_od_ref, w_om_ref, g_ffn_ref, w_up_ref,
                w_conv_ref, b_conv_ref, w_down_ref, g_fin_ref,
                y_ref, nconv_ref, carry_ref, gbuf_ref, act_ref, *, tm):
    t = pl.program_id(1)

    @pl.when(t == 0)
    def _():
        carry_ref[...] = jnp.zeros(carry_ref.shape, _F32)
        carry_ref[8 - (CONV_W - 1):, :] = cprev_ref[0]

    x1 = x_ref[0] + _dot(md_ref[0], w_od_ref[...]) + _dot(mm_ref[0], w_om_ref[...])
    h2 = _rms(x1, g_ffn_ref[...]).astype(_BF16)
    for c in range(D_FF // FF_CHUNK):
        cs = slice(c * FF_CHUNK, (c + 1) * FF_CHUNK)
        u = _dot(h2, w_up_ref[:, cs])
        g = _dot(h2, w_up_ref[:, D_FF + c * FF_CHUNK:D_FF + (c + 1) * FF_CHUNK])
        gbuf_ref[0:8, :] = carry_ref[:, cs]
        gbuf_ref[8:, :] = g
        conv = (b_conv_ref[:, cs] + gbuf_ref[6:6 + tm, :] * w_conv_ref[0:1, cs]
                + gbuf_ref[7:7 + tm, :] * w_conv_ref[1:2, cs] + g * w_conv_ref[2:3, cs])
        act_ref[:, cs] = (conv * jax.nn.sigmoid(conv) * u).astype(_BF16)
        carry_ref[:, cs] = g[tm - 8:, :]
    nconv_ref[0] = carry_ref[8 - (CONV_W - 1):, :]
    y_ref[0] = _rms(x1 + _dot(act_ref[...], w_down_ref[...]), g_fin_ref[...])


def _ffn_call(x, mix_d, mix_m, conv_prev, w_od, w_om, g_ffn, w_up, w_conv, b_conv, w_down,
              g_fin, tm):
    B, T, _ = x.shape
    row = lambda c: pl.BlockSpec((1, tm, c), lambda b, t: (b, t, 0))
    full = lambda a: pl.BlockSpec(a.shape, lambda b, t: (0,) * a.ndim,
                                  pipeline_mode=pl.Buffered(1))
    state = pl.BlockSpec((1, CONV_W - 1, D_FF), lambda b, t: (b, 0, 0))
    return pl.pallas_call(
        functools.partial(_ffn_kernel, tm=tm),
        grid=(B, T // tm),
        in_specs=[row(D_MODEL), row(DIFF_WIDTH), row(MLA_WIDTH), state, full(w_od), full(w_om),
                  full(g_ffn), full(w_up), full(w_conv), full(b_conv), full(w_down),
                  full(g_fin)],
        out_specs=(row(D_MODEL), state),
        out_shape=(jax.ShapeDtypeStruct((B, T, D_MODEL), _F32),
                   jax.ShapeDtypeStruct((B, CONV_W - 1, D_FF), _F32)),
        scratch_shapes=[pltpu.VMEM((8, D_FF), _F32), pltpu.VMEM((tm + 8, FF_CHUNK), _F32),
                        pltpu.VMEM((tm, D_FF), _BF16)],
        compiler_params=pltpu.CompilerParams(
            dimension_semantics=("arbitrary", "arbitrary"), vmem_limit_bytes=VMEM_LIMIT),
        name="ffn",
    )(x, mix_d, mix_m, conv_prev, w_od, w_om, g_ffn, w_up, w_conv, b_conv, w_down, g_fin)


def _rope_tables(pos):
    half = DIFF_HEAD_DIM // 2
    inv = ROPE_THETA ** (-jnp.arange(half, dtype=_F32) * (2.0 / DIFF_HEAD_DIM))
    ang = pos.astype(_F32)[:, None] * inv[None, :]
    cos, sin, zero = jnp.cos(ang), jnp.sin(ang), jnp.zeros_like(ang)
    cos2 = jnp.concatenate([cos, cos, cos, cos], axis=1)
    sin_lo = jnp.concatenate([-sin, zero, -sin, zero], axis=1)
    sin_hi = jnp.concatenate([zero, sin, zero, sin], axis=1)
    return cos2, sin_lo, sin_hi


def _layer(x, pos, past, lam_init, wl, g_final, *, tm, tq_d, tk_d, tq_m, tk_m):
    (g_attn, w_in_p, lam_p, g_sub, g_qa, w_qn, w_qr, wkt, wv, g_kva, w_od, w_om, g_ffn, w_up,
     w_conv, b_conv, w_down) = wl
    B, T, _ = x.shape
    tabs = _rope_tables(pos)
    qd, kd32, kd16, vd32, vd16, ckv32, kr32, kmla, vmla, qmla = _proj_call(
        x, tabs, g_attn, w_in_p, g_qa, w_qn, w_qr, wkt, g_kva, tm)

    if past is None:
        kd_all, vd_all, kmla_all, vmla_all = kd16, vd16, kmla, vmla
        conv_prev = jnp.zeros((B, CONV_W - 1, D_FF), _F32)
        n_keys, q_pos0 = T, 0
    else:
        p_dk, p_dv, p_ckv, p_kr, conv_prev = past
        P = p_dk.shape[1]
        n_keys, q_pos0 = P + T, P
        assert tk_d == tk_m and tk_d >= n_keys
        one = lambda *shape: jnp.ones(shape, _F32)

        def cat(old, new):
            a = jnp.concatenate([old.astype(_BF16), new], axis=1)
            return jnp.pad(a, ((0, 0), (0, tk_d - n_keys), (0, 0)))

        kd_all = cat(p_dk.reshape(B, P, DIFF_WIDTH), kd16)
        vd_all = cat(jnp.concatenate([p_dv, one(B, P, DIFF_HEADS, LANES)], axis=-1)
                     .reshape(B, P, DIFF_HEADS * V_ONES), vd16)
        kmla_all = cat(jnp.concatenate([p_ckv, p_kr, p_kr], axis=-1), kmla)
        vmla_all = cat(jnp.concatenate([p_ckv, one(B, P, LANES)], axis=-1), vmla)

    mix_d = _diff_attn_call(qd, kd_all, vd_all, lam_p, g_sub, tq=tq_d, tk=tk_d, q_pos0=q_pos0,
                            n_keys=n_keys, lam_init=lam_init)
    mix_m = _mla_attn_call(qmla, kmla_all, vmla_all, wv, tq=tq_m, tk=tk_m, q_pos0=q_pos0,
                           n_keys=n_keys)
    y, new_conv = _ffn_call(x, mix_d, mix_m, conv_prev, w_od, w_om, g_ffn, w_up, w_conv, b_conv,
                            w_down, g_final, tm)
    state = (kd32.reshape(1, B, T, DIFF_HEADS, 2, DIFF_HEAD_DIM),
             vd32.reshape(1, B, T, DIFF_HEADS, 2 * DIFF_HEAD_DIM),
             ckv32[None], kr32[None], new_conv[None])
    return y, state


def kernel(x_prompt, x_sample, cache_diff_k, cache_diff_v, cache_mla_ckv, cache_mla_krope,
           state_conv, g_attn, w_in, lambda_q1, lambda_k1, lambda_q2, lambda_k2, g_diff_sub,
           g_q_lora, w_q_b, g_kv_lora, w_kv_b, w_out, g_ffn, w_up, w_conv, b_conv, w_down,
           g_final):
    assert g_attn.shape[0] == 1, "single-layer model"
    S = x_prompt.shape[1]
    T = x_sample.shape[1]
    P = cache_diff_k.shape[2]
    lam_init = 0.8 - 0.6 * math.exp(-0.3 * 0)

    w_in_p = jnp.concatenate([w_in[0], w_in[0][:, OFF_KR:IN_COLS]], axis=1).astype(_BF16)
    wq = w_q_b[0].reshape(MLA_Q_LORA, MLA_HEADS, MLA_NOPE + MLA_ROPE)
    w_qn = wq[:, :, :MLA_NOPE].reshape(MLA_Q_LORA, MLA_HEADS * MLA_NOPE).astype(_BF16)
    w_qr = wq[:, :, MLA_NOPE:].reshape(MLA_Q_LORA, MLA_HEADS * MLA_ROPE).astype(_BF16)
    wkv = w_kv_b[0].reshape(MLA_KV_LORA, MLA_HEADS, MLA_NOPE + MLA_V)
    wkt = jnp.transpose(wkv[:, :, :MLA_NOPE], (1, 2, 0)).astype(_BF16)
    wv = jnp.transpose(wkv[:, :, MLA_NOPE:], (1, 0, 2)).astype(_BF16)
    lam_p = jnp.concatenate([lambda_q1, lambda_k1, lambda_q2, lambda_k2], axis=0)
    wl = (g_attn, w_in_p, lam_p, g_diff_sub, g_q_lora, w_qn, w_qr, wkt, wv, g_kv_lora,
          w_out[0][:DIFF_WIDTH].astype(_BF16), w_out[0][DIFF_WIDTH:].astype(_BF16), g_ffn,
          w_up[0].astype(_BF16), w_conv[0], b_conv, w_down[0].astype(_BF16))
    g_fin = g_final[None]

    pos_p = jnp.arange(S, dtype=jnp.int32)
    pos_s = P + jnp.arange(T, dtype=jnp.int32)
    y_p, st_p = _layer(x_prompt, pos_p, None, lam_init, wl, g_fin,
                       tm=512, tq_d=512, tk_d=512, tq_m=256, tk_m=512)
    past = (cache_diff_k[0], cache_diff_v[0], cache_mla_ckv[0], cache_mla_krope[0], state_conv[0])
    tk_s = -(-(P + T) // LANES) * LANES
    y_s, st_s = _layer(x_sample, pos_s, past, lam_init, wl, g_fin,
                       tm=T, tq_d=T, tk_d=tk_s, tq_m=T, tk_m=tk_s)
    return (y_p, y_s) + st_p + st_s
```

```python
import functools
import math

import jax
import jax.numpy as jnp
import numpy as np
from jax import lax
from jax.experimental import pallas as pl
from jax.experimental.pallas import tpu as pltpu

D_MODEL = 1024
CHUNK = 64
ROPE_THETA = 10000.0
NORM_EPS = 1e-6
NEG_INF = -1e30

DIFF_HEADS = 4
DIFF_HEAD_DIM = 64
DIFF_WIDTH = DIFF_HEADS * 2 * DIFF_HEAD_DIM
MLA_HEADS = 4
MLA_Q_LORA = 256
MLA_KV_LORA = 128
MLA_NOPE = 128
MLA_ROPE = 64
MLA_V = 128
MLA_WIDTH = MLA_HEADS * MLA_V
MLA_SCALE = (MLA_NOPE + MLA_ROPE) ** -0.5
OFF_DQ = 0
OFF_DK = OFF_DQ + DIFF_WIDTH
OFF_DV = OFF_DK + DIFF_WIDTH
OFF_CQ = OFF_DV + DIFF_WIDTH
OFF_CKV = OFF_CQ + MLA_Q_LORA
OFF_KR = OFF_CKV + MLA_KV_LORA
IN_COLS = OFF_KR + MLA_ROPE
IN_COLS_PAD = IN_COLS + MLA_ROPE
D_FF = 2816
CONV_W = 3

LANES = 128
MASK_COLS = 64
QK_W = 2 * LANES
MLA_QK = 2 * LANES
V_ONES = 2 * LANES
FF_CHUNK = 256
ITEMS_PER_TRIP = 4
LOG2E = math.log2(math.e)
VMEM_LIMIT = 56 * 1024 * 1024

_BF16 = jnp.bfloat16
_F32 = jnp.float32


def _dot(a, b):
    return jnp.dot(a, b, preferred_element_type=_F32)


def _dot_t(a, b):
    return lax.dot_general(a, b, (((1,), (1,)), ((), ())), preferred_element_type=_F32)


def _rms(xf, g):
    return xf * lax.rsqrt(jnp.mean(xf * xf, axis=-1, keepdims=True) + NORM_EPS) * g


def _rope_slab(xs, cos2, sin_lo, sin_hi):
    return (xs * cos2 + pltpu.roll(xs, LANES - 32, axis=1) * sin_lo
            + pltpu.roll(xs, 32, axis=1) * sin_hi)


def _chunk_mask_cols(pos):
    lane = lax.broadcasted_iota(jnp.int32, (1, LANES), 1)
    chunk = lax.shift_right_logical(pos, CHUNK.bit_length() - 1)
    q_cols = jnp.where(lane == chunk, 1.0, 0.0)
    k_cols = jnp.where((chunk > lane) & (lane < MASK_COLS), NEG_INF, 0.0)
    return q_cols, k_cols


def _proj_kernel(x_ref, cos_ref, slo_ref, shi_ref, g_attn_ref, w_in_ref, g_qa_ref, w_qn_ref,
                 w_qr_ref, wkt_ref, g_kva_ref,
                 qd_ref, kd32_ref, kd16_ref, vd32_ref, vd16_ref, ckv32_ref, kr32_ref,
                 kmla_ref, vmla_ref, qmla_ref, *, tm, q_pos0):
    x = x_ref[0]
    h = _rms(x, g_attn_ref[...]).astype(_BF16)
    proj = _dot(h, w_in_ref[...])
    cos2, slo, shi = cos_ref[...], slo_ref[...], shi_ref[...]
    rope = functools.partial(_rope_slab, cos2=cos2, sin_lo=slo, sin_hi=shi)
    ones = jnp.ones((tm, LANES), _BF16)
    pos = q_pos0 + pl.program_id(1) * tm + lax.broadcasted_iota(jnp.int32, (tm, 1), 0)
    q_cols, k_cols = _chunk_mask_cols(pos)
    lane = lax.broadcasted_iota(jnp.int32, (1, LANES), 1)

    q_scale = DIFF_HEAD_DIM ** -0.5 * LOG2E
    for s in range(DIFF_HEADS):
        cs = slice(s * LANES, (s + 1) * LANES)
        qd_ref[0, :, s * QK_W:s * QK_W + LANES] = (
            rope(proj[:, OFF_DQ + s * LANES:OFF_DQ + (s + 1) * LANES]) * q_scale).astype(_BF16)
        qd_ref[0, :, s * QK_W + LANES:(s + 1) * QK_W] = q_cols.astype(_BF16)
        k = rope(proj[:, OFF_DK + s * LANES:OFF_DK + (s + 1) * LANES])
        kd32_ref[0, :, cs] = k
        kd16_ref[0, :, s * QK_W:s * QK_W + LANES] = k.astype(_BF16)
        kd16_ref[0, :, s * QK_W + LANES:(s + 1) * QK_W] = k_cols.astype(_BF16)
        v = proj[:, OFF_DV + s * LANES:OFF_DV + (s + 1) * LANES]
        vd32_ref[0, :, cs] = v
        vd16_ref[0, :, s * V_ONES:s * V_ONES + LANES] = v.astype(_BF16)
        vd16_ref[0, :, s * V_ONES + LANES:(s + 1) * V_ONES] = ones

    c_q = _rms(proj[:, OFF_CQ:OFF_CKV], g_qa_ref[...]).astype(_BF16)
    c_kv = _rms(proj[:, OFF_CKV:OFF_KR], g_kva_ref[...])
    k_r2 = rope(proj[:, OFF_KR:IN_COLS_PAD])
    ckv32_ref[0] = c_kv
    kr32_ref[0] = k_r2[:, :MLA_ROPE]
    kmla_ref[0, :, :LANES] = c_kv.astype(_BF16)
    kmla_ref[0, :, LANES:] = jnp.where(lane < MLA_ROPE, k_r2,
                                       pltpu.roll(k_cols, MLA_ROPE, axis=1)).astype(_BF16)
    vmla_ref[0, :, :LANES] = c_kv.astype(_BF16)
    vmla_ref[0, :, LANES:] = ones

    m_scale = MLA_SCALE * LOG2E
    q_nope = _dot(c_q, w_qn_ref[...]).astype(_BF16)
    q_rope = _dot(c_q, w_qr_ref[...])
    q_cols_hi = pltpu.roll(q_cols, MLA_ROPE, axis=1)
    for hd in range(MLA_HEADS):
        q_lat = _dot(q_nope[:, hd * MLA_NOPE:(hd + 1) * MLA_NOPE], wkt_ref[hd])
        qmla_ref[0, hd, :, :LANES] = (q_lat * m_scale).astype(_BF16)
        if hd % 2 == 0:
            slab = rope(q_rope[:, (hd // 2) * LANES:(hd // 2 + 1) * LANES]) * m_scale
        else:
            slab = pltpu.roll(slab, MLA_ROPE, axis=1)
        qmla_ref[0, hd, :, LANES:] = jnp.where(lane < MLA_ROPE, slab, q_cols_hi).astype(_BF16)


def _proj_call(x, tabs, g_attn, w_in_p, g_qa, w_qn, w_qr, wkt, g_kva, *, tm, q_pos0):
    B, T, _ = x.shape
    nt = T // tm
    assert -(-(q_pos0 + T) // CHUNK) <= MASK_COLS, "chunk one-hot must fit the mask columns"
    row = lambda c: pl.BlockSpec((1, tm, c), lambda b, t: (b, t, 0))
    tab = pl.BlockSpec((tm, LANES), lambda b, t: (t, 0))
    full = lambda a: pl.BlockSpec(a.shape, lambda b, t: (0,) * a.ndim)
    outs = (
        (DIFF_HEADS * QK_W, _BF16),
        (DIFF_WIDTH, _F32),
        (DIFF_HEADS * QK_W, _BF16),
        (DIFF_WIDTH, _F32),
        (DIFF_HEADS * V_ONES, _BF16),
        (MLA_KV_LORA, _F32),
        (MLA_ROPE, _F32),
        (MLA_QK, _BF16),
        (V_ONES, _BF16),
    )
    out_specs = tuple(row(c) for c, _ in outs) + (
        pl.BlockSpec((1, MLA_HEADS, tm, MLA_QK), lambda b, t: (b, 0, t, 0)),)
    out_shape = tuple(jax.ShapeDtypeStruct((B, T, c), d) for c, d in outs) + (
        jax.ShapeDtypeStruct((B, MLA_HEADS, T, MLA_QK), _BF16),)
    return pl.pallas_call(
        functools.partial(_proj_kernel, tm=tm, q_pos0=q_pos0),
        grid=(B, nt),
        in_specs=[row(D_MODEL), tab, tab, tab, full(g_attn), full(w_in_p), full(g_qa),
                  full(w_qn), full(w_qr), full(wkt), full(g_kva)],
        out_specs=out_specs,
        out_shape=out_shape,
        compiler_params=pltpu.CompilerParams(
            dimension_semantics=("arbitrary", "arbitrary"), vmem_limit_bytes=VMEM_LIMIT),
        name="proj",
    )(x, *tabs, g_attn, w_in_p, g_qa, w_qn, w_qr, wkt, g_kva)


def _attn_schedule(T, tq, tk, q_pos0, n_keys):
    items = []
    for qi in range(T // tq):
        last_pos = q_pos0 + (qi + 1) * tq - 1
        n_kv = -(-min((last_pos // CHUNK + 1) * CHUNK, n_keys) // tk)
        items += [(qi, kj, int(kj == 0), int(kj == n_kv - 1), qi % ITEMS_PER_TRIP)
                  for kj in range(n_kv)]
    trips = -(-len(items) // ITEMS_PER_TRIP)
    items += [(0, 0, 1, 0, ITEMS_PER_TRIP)] * (ITEMS_PER_TRIP * trips + 1 - len(items))
    return np.asarray(items, np.int32).T.copy(), trips


def _flash(tab_ref, load_q, k_ref, v_ref, emit, scratch, *, trips, first_step):
    (s0, s1, m0, m1, a0, a1, acc_ref) = scratch
    tk = s0.shape[1]

    def score(w, s_ref, m_ref, a_ref, m_prev_ref):
        k0 = pl.multiple_of(tab_ref[1, w] * tk, tk)
        s = _dot_t(load_q(tab_ref[0, w]), k_ref[0, pl.ds(k0, tk), :])
        m_prev = jnp.where(tab_ref[2, w] > 0, NEG_INF, m_prev_ref[...])
        m_new = jnp.maximum(m_prev, jnp.max(s, axis=1, keepdims=True))
        s_ref[...] = s
        a_ref[...] = jnp.exp2(m_prev - m_new)
        m_ref[...] = m_new

    def finish(w, s_ref, m_ref, a_ref):
        k0 = pl.multiple_of(tab_ref[1, w] * tk, tk)
        slot = tab_ref[4, w]
        p = jnp.exp2(s_ref[...] - jnp.tile(m_ref[...], (1, tk // LANES)))
        pv = _dot(p.astype(_BF16), v_ref[0, pl.ds(k0, tk), :])
        acc_ref[slot] = jnp.tile(a_ref[...], (1, V_ONES // LANES)) * acc_ref[slot] + pv

    def emit_if_last(w):
        @pl.when(tab_ref[3, w] > 0)
        def _():
            acc = acc_ref[tab_ref[4, w]]
            emit(tab_ref[0, w], acc[:, :LANES] / acc[:, LANES:])

    @pl.when(first_step)
    def _():
        acc_ref[...] = jnp.zeros(acc_ref.shape, _F32)

    m1[...] = jnp.full(m1.shape, NEG_INF, _F32)
    score(0, s0, m0, a0, m1)

    even, odd = (s0, m0, a0), (s1, m1, a1)

    def trip(t, carry):
        w = ITEMS_PER_TRIP * t
        for i in range(0, ITEMS_PER_TRIP, 2):
            score(w + i + 1, *odd, m0)
            finish(w + i, *even)
            score(w + i + 2, *even, m1)
            finish(w + i + 1, *odd)
        for i in range(ITEMS_PER_TRIP):
            emit_if_last(w + i)
        return carry

    lax.fori_loop(0, trips, trip, 0)


def _attn_scratch(rows, tk):
    s = pltpu.VMEM((rows, tk), _F32)
    r = pltpu.VMEM((rows, LANES), _F32)
    return [s, s, r, r, r, r, pltpu.VMEM((ITEMS_PER_TRIP + 1, rows, V_ONES), _F32)]


def _diff_attn_kernel(tab_ref, q_ref, k_ref, v_ref, lam_ref, o_ref, *scratch, tq, trips,
                      lam_init):
    lane = lax.broadcasted_iota(jnp.int32, (1, QK_W), 1)
    lp = lam_ref[...]
    lam = (jnp.exp(jnp.sum(lp[0:1] * lp[1:2], axis=1, keepdims=True))
           - jnp.exp(jnp.sum(lp[2:3] * lp[3:4], axis=1, keepdims=True)) + lam_init)

    def load_q(qi):
        q = q_ref[0, pl.ds(pl.multiple_of(qi * tq, tq), tq), :]
        zero = jnp.zeros_like(q)
        return jnp.concatenate(
            [jnp.where((lane < DIFF_HEAD_DIM) | (lane >= LANES), q, zero),
             jnp.where(lane >= DIFF_HEAD_DIM, q, zero)], axis=0)

    def emit(qi, o):
        o_ref[0, pl.ds(pl.multiple_of(qi * tq, tq), tq), :] = (
            o[:tq] - lam * o[tq:]).astype(o_ref.dtype)

    first_step = (pl.program_id(0) == 0) & (pl.program_id(1) == 0)
    _flash(tab_ref, load_q, k_ref, v_ref, emit, scratch, trips=trips, first_step=first_step)


def _diff_attn_call(qd, kd, vd, lam_p, *, tq, tk, q_pos0, n_keys, lam_init):
    B, T, _ = qd.shape
    Sk = kd.shape[1]
    tab, trips = _attn_schedule(T, tq, tk, q_pos0, n_keys)
    kern = functools.partial(_diff_attn_kernel, tq=tq, trips=trips, lam_init=lam_init)
    head = lambda rows, c: pl.BlockSpec((1, rows, c), lambda b, h, tab: (b, 0, h))
    return pl.pallas_call(
        kern,
        grid_spec=pltpu.PrefetchScalarGridSpec(
            num_scalar_prefetch=1,
            grid=(B, DIFF_HEADS),
            in_specs=[head(T, QK_W), head(Sk, QK_W), head(Sk, V_ONES),
                      pl.BlockSpec(lam_p.shape, lambda b, h, tab: (0, 0))],
            out_specs=head(T, LANES),
            scratch_shapes=_attn_scratch(2 * tq, tk)),
        out_shape=jax.ShapeDtypeStruct((B, T, DIFF_WIDTH), _BF16),
        compiler_params=pltpu.CompilerParams(
            dimension_semantics=("arbitrary",) * 2, vmem_limit_bytes=VMEM_LIMIT),
        name="diff_attn",
    )(jnp.asarray(tab), qd, kd, vd, lam_p)


def _mla_attn_kernel(tab_ref, q_ref, k_ref, v_ref, o_ref, *scratch, tq, trips):
    def load_q(qi):
        q = q_ref[0, :, pl.ds(pl.multiple_of(qi * tq, tq), tq), :]
        return q.reshape(MLA_HEADS * tq, MLA_QK)

    def emit(qi, o):
        o = o.astype(o_ref.dtype)
        for h in range(MLA_HEADS):
            o_ref[0, pl.ds(pl.multiple_of(qi * tq, tq), tq), h * LANES:(h + 1) * LANES] = (
                o[h * tq:(h + 1) * tq])

    _flash(tab_ref, load_q, k_ref, v_ref, emit, scratch, trips=trips,
           first_step=pl.program_id(0) == 0)


def _mla_attn_call(qm, kk, vv, *, tq, tk, q_pos0, n_keys):
    B, _, T, _ = qm.shape
    Sk = kk.shape[1]
    tab, trips = _attn_schedule(T, tq, tk, q_pos0, n_keys)
    kern = functools.partial(_mla_attn_kernel, tq=tq, trips=trips)
    kv = pl.BlockSpec((1, Sk, MLA_QK), lambda b, tab: (b, 0, 0))
    return pl.pallas_call(
        kern,
        grid_spec=pltpu.PrefetchScalarGridSpec(
            num_scalar_prefetch=1,
            grid=(B,),
            in_specs=[pl.BlockSpec((1, MLA_HEADS, T, MLA_QK), lambda b, tab: (b, 0, 0, 0)),
                      kv, kv],
            out_specs=pl.BlockSpec((1, T, MLA_HEADS * LANES), lambda b, tab: (b, 0, 0)),
            scratch_shapes=_attn_scratch(MLA_HEADS * tq, tk)),
        out_shape=jax.ShapeDtypeStruct((B, T, MLA_HEADS * LANES), _BF16),
        compiler_params=pltpu.CompilerParams(
            dimension_semantics=("arbitrary",), vmem_limit_bytes=VMEM_LIMIT),
        name="mla_attn",
    )(jnp.asarray(tab), qm, kk, vv)


def _ffn_kernel(x_ref, od_ref, om_ref, cprev_ref, gsub_ref, wv_ref, w_od_ref, w_om_ref,
                g_ffn_ref, w_up_ref, w_conv_ref, b_conv_ref, w_down_ref, g_fin_ref,
                y_ref, nconv_ref, carry_ref, gbuf_ref, act_ref, *, tm, lam_init):
    t = pl.program_id(1)

    @pl.when(t == 0)
    def _():
        carry_ref[...] = jnp.zeros(carry_ref.shape, _F32)
        carry_ref[8 - (CONV_W - 1):, :] = cprev_ref[0]

    mix_d = jnp.concatenate(
        [(_rms(od_ref[0, :, h * LANES:(h + 1) * LANES].astype(_F32), gsub_ref[...])
          * (1.0 - lam_init)).astype(_BF16) for h in range(DIFF_HEADS)], axis=1)
    mix_m = jnp.concatenate(
        [_dot(om_ref[0, :, h * LANES:(h + 1) * LANES], wv_ref[h]).astype(_BF16)
         for h in range(MLA_HEADS)], axis=1)
    x1 = x_ref[0] + _dot(mix_d, w_od_ref[...]) + _dot(mix_m, w_om_ref[...])
    h2 = _rms(x1, g_ffn_ref[...]).astype(_BF16)
    for c in range(D_FF // FF_CHUNK):
        cs = slice(c * FF_CHUNK, (c + 1) * FF_CHUNK)
        u = _dot(h2, w_up_ref[:, cs])
        g = _dot(h2, w_up_ref[:, D_FF + c * FF_CHUNK:D_FF + (c + 1) * FF_CHUNK])
        gbuf_ref[0:8, :] = carry_ref[:, cs]
        gbuf_ref[8:, :] = g
        conv = (b_conv_ref[:, cs] + gbuf_ref[6:6 + tm, :] * w_conv_ref[0:1, cs]
                + gbuf_ref[7:7 + tm, :] * w_conv_ref[1:2, cs] + g * w_conv_ref[2:3, cs])
        act_ref[:, cs] = (conv * jax.nn.sigmoid(conv) * u).astype(_BF16)
        carry_ref[:, cs] = g[tm - 8:, :]
    nconv_ref[0] = carry_ref[8 - (CONV_W - 1):, :]
    y_ref[0] = _rms(x1 + _dot(act_ref[...], w_down_ref[...]), g_fin_ref[...])


def _ffn_call(x, o_d, o_m, conv_prev, g_sub, wv, w_od, w_om, g_ffn, w_up, w_conv, b_conv,
              w_down, g_fin, *, tm, lam_init):
    B, T, _ = x.shape
    row = lambda c: pl.BlockSpec((1, tm, c), lambda b, t: (b, t, 0))
    full = lambda a: pl.BlockSpec(a.shape, lambda b, t: (0,) * a.ndim,
                                  pipeline_mode=pl.Buffered(1))
    state = pl.BlockSpec((1, CONV_W - 1, D_FF), lambda b, t: (b, 0, 0))
    weights = (g_sub, wv, w_od, w_om, g_ffn, w_up, w_conv, b_conv, w_down, g_fin)
    return pl.pallas_call(
        functools.partial(_ffn_kernel, tm=tm, lam_init=lam_init),
        grid=(B, T // tm),
        in_specs=[row(D_MODEL), row(DIFF_WIDTH), row(MLA_WIDTH), state]
                 + [full(w) for w in weights],
        out_specs=(row(D_MODEL), state),
        out_shape=(jax.ShapeDtypeStruct((B, T, D_MODEL), _F32),
                   jax.ShapeDtypeStruct((B, CONV_W - 1, D_FF), _F32)),
        scratch_shapes=[pltpu.VMEM((8, D_FF), _F32), pltpu.VMEM((tm + 8, FF_CHUNK), _F32),
                        pltpu.VMEM((tm, D_FF), _BF16)],
        compiler_params=pltpu.CompilerParams(
            dimension_semantics=("arbitrary", "arbitrary"), vmem_limit_bytes=VMEM_LIMIT),
        name="ffn",
    )(x, o_d, o_m, conv_prev, *weights)


def _rope_tables(pos):
    half = DIFF_HEAD_DIM // 2
    inv = ROPE_THETA ** (-jnp.arange(half, dtype=_F32) * (2.0 / DIFF_HEAD_DIM))
    ang = pos.astype(_F32)[:, None] * inv[None, :]
    cos, sin, zero = jnp.cos(ang), jnp.sin(ang), jnp.zeros_like(ang)
    cos2 = jnp.concatenate([cos, cos, cos, cos], axis=1)
    sin_lo = jnp.concatenate([-sin, zero, -sin, zero], axis=1)
    sin_hi = jnp.concatenate([zero, sin, zero, sin], axis=1)
    return cos2, sin_lo, sin_hi


def _tiles(T, n_keys):
    if T >= 512:
        return dict(tm=512, diff=(512, 512), mla=(256, 512))
    tk = -(-n_keys // LANES) * LANES
    return dict(tm=T, diff=(T, tk), mla=(T, tk))


def _layer(x, pos, past, lam_init, wl, g_final):
    (g_attn, w_in_p, lam_p, g_sub, g_qa, w_qn, w_qr, wkt, wv, g_kva, w_od, w_om, g_ffn, w_up,
     w_conv, b_conv, w_down) = wl
    B, T, _ = x.shape
    n_past = 0 if past is None else past[0].shape[1]
    n_keys, q_pos0 = n_past + T, n_past
    tl = _tiles(T, n_keys)
    tabs = _rope_tables(pos)
    qd, kd32, kd16, vd32, vd16, ckv32, kr32, kmla, vmla, qmla = _proj_call(
        x, tabs, g_attn, w_in_p, g_qa, w_qn, w_qr, wkt, g_kva, tm=tl["tm"], q_pos0=q_pos0)

    if past is None:
        kd_all, vd_all, kmla_all, vmla_all = kd16, vd16, kmla, vmla
        conv_prev = jnp.zeros((B, CONV_W - 1, D_FF), _F32)
    else:
        p_dk, p_dv, p_ckv, p_kr, conv_prev = past
        n_pad = tl["diff"][1] - n_keys
        lane = jnp.arange(LANES)[None, :]
        chunk = jnp.concatenate([jnp.arange(n_past) // CHUNK, jnp.full((n_pad,), MASK_COLS)])
        k_cols = jnp.where((chunk[:, None] > lane) & (lane < MASK_COLS), NEG_INF, 0.0)

        def cat(old, new, width):
            rows = jnp.pad(jnp.concatenate(old, axis=-1), ((0, 0), (0, n_pad)) + ((0, 0),) * (
                old[0].ndim - 2))
            if width:
                cols = jnp.broadcast_to(k_cols[:, :width].reshape(
                    (1, -1) + (1,) * (rows.ndim - 3) + (width,)), rows.shape[:-1] + (width,))
                rows = jnp.concatenate([rows, cols], axis=-1)
            rows = rows.reshape(B, n_past + n_pad, -1).astype(_BF16)
            return jnp.concatenate([rows[:, :n_past], new, rows[:, n_past:]], axis=1)

        one = lambda *shape: jnp.ones(shape, _F32)
        kd_all = cat([p_dk.reshape(B, n_past, DIFF_HEADS, LANES)], kd16, LANES)
        vd_all = cat([p_dv, one(B, n_past, DIFF_HEADS, LANES)], vd16, 0)
        kmla_all = cat([p_ckv, p_kr], kmla, MASK_COLS)
        vmla_all = cat([p_ckv, one(B, n_past, LANES)], vmla, 0)

    o_d = _diff_attn_call(qd, kd_all, vd_all, lam_p, tq=tl["diff"][0], tk=tl["diff"][1],
                          q_pos0=q_pos0, n_keys=n_keys, lam_init=lam_init)
    o_m = _mla_attn_call(qmla, kmla_all, vmla_all, tq=tl["mla"][0], tk=tl["mla"][1],
                         q_pos0=q_pos0, n_keys=n_keys)
    y, new_conv = _ffn_call(x, o_d, o_m, conv_prev, g_sub, wv, w_od, w_om, g_ffn, w_up, w_conv,
                            b_conv, w_down, g_final, tm=tl["tm"], lam_init=lam_init)
    state = (kd32.reshape(1, B, T, DIFF_HEADS, 2, DIFF_HEAD_DIM),
             vd32.reshape(1, B, T, DIFF_HEADS, 2 * DIFF_HEAD_DIM),
             ckv32[None], kr32[None], new_conv[None])
    return y, state


def kernel(x_prompt, x_sample, cache_diff_k, cache_diff_v, cache_mla_ckv, cache_mla_krope,
           state_conv, g_attn, w_in, lambda_q1, lambda_k1, lambda_q2, lambda_k2, g_diff_sub,
           g_q_lora, w_q_b, g_kv_lora, w_kv_b, w_out, g_ffn, w_up, w_conv, b_conv, w_down,
           g_final):
    assert g_attn.shape[0] == 1, "single-layer model"
    S = x_prompt.shape[1]
    T = x_sample.shape[1]
    P = cache_diff_k.shape[2]
    lam_init = 0.8 - 0.6 * math.exp(-0.3 * 0)

    w_in_p = jnp.concatenate([w_in[0], w_in[0][:, OFF_KR:IN_COLS]], axis=1).astype(_BF16)
    wq = w_q_b[0].reshape(MLA_Q_LORA, MLA_HEADS, MLA_NOPE + MLA_ROPE)
    w_qn = wq[:, :, :MLA_NOPE].reshape(MLA_Q_LORA, MLA_HEADS * MLA_NOPE).astype(_BF16)
    w_qr = wq[:, :, MLA_NOPE:].reshape(MLA_Q_LORA, MLA_HEADS * MLA_ROPE).astype(_BF16)
    wkv = w_kv_b[0].reshape(MLA_KV_LORA, MLA_HEADS, MLA_NOPE + MLA_V)
    wkt = jnp.transpose(wkv[:, :, :MLA_NOPE], (1, 2, 0)).astype(_BF16)
    wv = jnp.transpose(wkv[:, :, MLA_NOPE:], (1, 0, 2)).astype(_BF16)
    lam_p = jnp.concatenate([lambda_q1, lambda_k1, lambda_q2, lambda_k2], axis=0)
    wl = (g_attn, w_in_p, lam_p, g_diff_sub, g_q_lora, w_qn, w_qr, wkt, wv, g_kv_lora,
          w_out[0][:DIFF_WIDTH].astype(_BF16), w_out[0][DIFF_WIDTH:].astype(_BF16), g_ffn,
          w_up[0].astype(_BF16), w_conv[0], b_conv, w_down[0].astype(_BF16))
    g_fin = g_final[None]

    pos_p = jnp.arange(S, dtype=jnp.int32)
    pos_s = P + jnp.arange(T, dtype=jnp.int32)
    y_p, st_p = _layer(x_prompt, pos_p, None, lam_init, wl, g_fin)
    past = (cache_diff_k[0], cache_diff_v[0], cache_mla_ckv[0], cache_mla_krope[0], state_conv[0])
    y_s, st_s = _layer(x_sample, pos_s, past, lam_init, wl, g_fin)
    return (y_p, y_s) + st_p + st_s
```

```python
import functools
import math

import jax
import jax.numpy as jnp
import numpy as np
from jax import lax
from jax.experimental import pallas as pl
from jax.experimental.pallas import tpu as pltpu

D_MODEL = 1024
CHUNK = 64
ROPE_THETA = 10000.0
NORM_EPS = 1e-6
NEG_INF = -1e30

DIFF_HEADS = 4
DIFF_HEAD_DIM = 64
DIFF_WIDTH = DIFF_HEADS * 2 * DIFF_HEAD_DIM
MLA_HEADS = 4
MLA_Q_LORA = 256
MLA_KV_LORA = 128
MLA_NOPE = 128
MLA_ROPE = 64
MLA_V = 128
MLA_WIDTH = MLA_HEADS * MLA_V
MLA_SCALE = (MLA_NOPE + MLA_ROPE) ** -0.5
OFF_DQ = 0
OFF_DK = OFF_DQ + DIFF_WIDTH
OFF_DV = OFF_DK + DIFF_WIDTH
OFF_CQ = OFF_DV + DIFF_WIDTH
OFF_CKV = OFF_CQ + MLA_Q_LORA
OFF_KR = OFF_CKV + MLA_KV_LORA
IN_COLS = OFF_KR + MLA_ROPE
IN_COLS_PAD = IN_COLS + MLA_ROPE
D_FF = 2816
CONV_W = 3

LANES = 128
MASK_COLS = 64
QK_W = 2 * LANES
MLA_QK = 2 * LANES
V_ONES = 2 * LANES
FF_CHUNK = 256
ITEMS_PER_TRIP = 4
LOG2E = math.log2(math.e)
VMEM_LIMIT = 56 * 1024 * 1024

_BF16 = jnp.bfloat16
_F32 = jnp.float32


def _dot(a, b):
    return jnp.dot(a, b, preferred_element_type=_F32)


def _dot_t(a, b):
    return lax.dot_general(a, b, (((1,), (1,)), ((), ())), preferred_element_type=_F32)


def _rms(xf, g):
    return xf * lax.rsqrt(jnp.mean(xf * xf, axis=-1, keepdims=True) + NORM_EPS) * g


def _rope_slab(xs, cos2, sin_lo, sin_hi):
    return (xs * cos2 + pltpu.roll(xs, LANES - 32, axis=1) * sin_lo
            + pltpu.roll(xs, 32, axis=1) * sin_hi)


def _chunk_mask_cols(pos):
    lane = lax.broadcasted_iota(jnp.int32, (1, LANES), 1)
    chunk = lax.shift_right_logical(pos, CHUNK.bit_length() - 1)
    q_cols = jnp.where(lane == chunk, 1.0, 0.0)
    k_cols = jnp.where((chunk > lane) & (lane < MASK_COLS), NEG_INF, 0.0)
    return q_cols, k_cols


def _proj_kernel(x_ref, cos_ref, slo_ref, shi_ref, g_attn_ref, w_in_ref, g_qa_ref, w_qn_ref,
                 w_qr_ref, wkt_ref, g_kva_ref,
                 qd_ref, kd32_ref, kd16_ref, vd32_ref, vd16_ref, ckv32_ref, kr32_ref,
                 kmla_ref, vmla_ref, qmla_ref, *, tm, q_pos0):
    x = x_ref[0]
    h = _rms(x, g_attn_ref[...]).astype(_BF16)
    cos2, slo, shi = cos_ref[...], slo_ref[...], shi_ref[...]
    rope = functools.partial(_rope_slab, cos2=cos2, sin_lo=slo, sin_hi=shi)
    ones = jnp.ones((tm, LANES), _BF16)
    pos = q_pos0 + pl.program_id(1) * tm + lax.broadcasted_iota(jnp.int32, (tm, 1), 0)
    q_cols, k_cols = _chunk_mask_cols(pos)
    lane = lax.broadcasted_iota(jnp.int32, (1, LANES), 1)

    lat = _dot(h, w_in_ref[:, OFF_CQ:IN_COLS_PAD])
    c_q = _rms(lat[:, :MLA_Q_LORA], g_qa_ref[...]).astype(_BF16)
    c_kv = _rms(lat[:, MLA_Q_LORA:OFF_KR - OFF_CQ], g_kva_ref[...])
    k_r2 = rope(lat[:, OFF_KR - OFF_CQ:])

    proj = _dot(h, w_in_ref[:, :OFF_CQ])
    q_scale = DIFF_HEAD_DIM ** -0.5 * LOG2E
    for s in range(DIFF_HEADS):
        cs = slice(s * LANES, (s + 1) * LANES)
        qd_ref[0, :, s * QK_W:s * QK_W + LANES] = (
            rope(proj[:, OFF_DQ + s * LANES:OFF_DQ + (s + 1) * LANES]) * q_scale).astype(_BF16)
        qd_ref[0, :, s * QK_W + LANES:(s + 1) * QK_W] = q_cols.astype(_BF16)
        k = rope(proj[:, OFF_DK + s * LANES:OFF_DK + (s + 1) * LANES])
        kd32_ref[0, :, cs] = k
        kd16_ref[0, :, s * QK_W:s * QK_W + LANES] = k.astype(_BF16)
        kd16_ref[0, :, s * QK_W + LANES:(s + 1) * QK_W] = k_cols.astype(_BF16)
        v = proj[:, OFF_DV + s * LANES:OFF_DV + (s + 1) * LANES]
        vd32_ref[0, :, cs] = v
        vd16_ref[0, :, s * V_ONES:s * V_ONES + LANES] = v.astype(_BF16)
        vd16_ref[0, :, s * V_ONES + LANES:(s + 1) * V_ONES] = ones

    ckv32_ref[0] = c_kv
    kr32_ref[0] = k_r2[:, :MLA_ROPE]
    kmla_ref[0, :, :LANES] = c_kv.astype(_BF16)
    kmla_ref[0, :, LANES:] = jnp.where(lane < MLA_ROPE, k_r2,
                                       pltpu.roll(k_cols, MLA_ROPE, axis=1)).astype(_BF16)
    vmla_ref[0, :, :LANES] = c_kv.astype(_BF16)
    vmla_ref[0, :, LANES:] = ones

    m_scale = MLA_SCALE * LOG2E
    q_nope = _dot(c_q, w_qn_ref[...]).astype(_BF16)
    q_rope = _dot(c_q, w_qr_ref[...])
    q_cols_hi = pltpu.roll(q_cols, MLA_ROPE, axis=1)
    for hd in range(MLA_HEADS):
        q_lat = _dot(q_nope[:, hd * MLA_NOPE:(hd + 1) * MLA_NOPE], wkt_ref[hd])
        qmla_ref[0, hd, :, :LANES] = (q_lat * m_scale).astype(_BF16)
        if hd % 2 == 0:
            slab = rope(q_rope[:, (hd // 2) * LANES:(hd // 2 + 1) * LANES]) * m_scale
        else:
            slab = pltpu.roll(slab, MLA_ROPE, axis=1)
        qmla_ref[0, hd, :, LANES:] = jnp.where(lane < MLA_ROPE, slab, q_cols_hi).astype(_BF16)


def _proj_call(x, tabs, g_attn, w_in_p, g_qa, w_qn, w_qr, wkt, g_kva, *, tm, q_pos0):
    B, T, _ = x.shape
    nt = T // tm
    assert -(-(q_pos0 + T) // CHUNK) <= MASK_COLS, "chunk one-hot must fit the mask columns"
    row = lambda c: pl.BlockSpec((1, tm, c), lambda b, t: (b, t, 0))
    tab = pl.BlockSpec((tm, LANES), lambda b, t: (t, 0))
    full = lambda a: pl.BlockSpec(a.shape, lambda b, t: (0,) * a.ndim)
    outs = (
        (DIFF_HEADS * QK_W, _BF16),
        (DIFF_WIDTH, _F32),
        (DIFF_HEADS * QK_W, _BF16),
        (DIFF_WIDTH, _F32),
        (DIFF_HEADS * V_ONES, _BF16),
        (MLA_KV_LORA, _F32),
        (MLA_ROPE, _F32),
        (MLA_QK, _BF16),
        (V_ONES, _BF16),
    )
    out_specs = tuple(row(c) for c, _ in outs) + (
        pl.BlockSpec((1, MLA_HEADS, tm, MLA_QK), lambda b, t: (b, 0, t, 0)),)
    out_shape = tuple(jax.ShapeDtypeStruct((B, T, c), d) for c, d in outs) + (
        jax.ShapeDtypeStruct((B, MLA_HEADS, T, MLA_QK), _BF16),)
    return pl.pallas_call(
        functools.partial(_proj_kernel, tm=tm, q_pos0=q_pos0),
        grid=(B, nt),
        in_specs=[row(D_MODEL), tab, tab, tab, full(g_attn), full(w_in_p), full(g_qa),
                  full(w_qn), full(w_qr), full(wkt), full(g_kva)],
        out_specs=out_specs,
        out_shape=out_shape,
        compiler_params=pltpu.CompilerParams(
            dimension_semantics=("arbitrary", "arbitrary"), vmem_limit_bytes=VMEM_LIMIT),
        name="proj",
    )(x, *tabs, g_attn, w_in_p, g_qa, w_qn, w_qr, wkt, g_kva)


def _attn_schedule(T, tq, tk, q_pos0, n_keys):
    items = []
    for qi in range(T // tq):
        last_pos = q_pos0 + (qi + 1) * tq - 1
        n_kv = -(-min((last_pos // CHUNK + 1) * CHUNK, n_keys) // tk)
        items += [(qi, kj, int(kj == 0), int(kj == n_kv - 1), qi % ITEMS_PER_TRIP)
                  for kj in range(n_kv)]
    trips = -(-len(items) // ITEMS_PER_TRIP)
    items += [(0, 0, 1, 0, ITEMS_PER_TRIP)] * (ITEMS_PER_TRIP * trips + 1 - len(items))
    return np.asarray(items, np.int32).T.copy(), trips


def _flash(tab_ref, load_q, k_ref, v_ref, emit, scratch, *, trips, first_step):
    (s0, s1, m0, m1, a0, a1, acc_ref) = scratch
    tk = s0.shape[1]

    def score(w, s_ref, m_ref, a_ref, m_prev_ref):
        k0 = pl.multiple_of(tab_ref[1, w] * tk, tk)
        s = _dot_t(load_q(tab_ref[0, w]), k_ref[0, pl.ds(k0, tk), :])
        m_prev = jnp.where(tab_ref[2, w] > 0, NEG_INF, m_prev_ref[...])
        m_new = jnp.maximum(m_prev, jnp.max(s, axis=1, keepdims=True))
        s_ref[...] = s
        a_ref[...] = jnp.exp2(m_prev - m_new)
        m_ref[...] = m_new

    def finish(w, s_ref, m_ref, a_ref):
        k0 = pl.multiple_of(tab_ref[1, w] * tk, tk)
        slot = tab_ref[4, w]
        p = jnp.exp2(s_ref[...] - jnp.tile(m_ref[...], (1, tk // LANES)))
        pv = _dot(p.astype(_BF16), v_ref[0, pl.ds(k0, tk), :])
        acc_ref[slot] = jnp.tile(a_ref[...], (1, V_ONES // LANES)) * acc_ref[slot] + pv

    def emit_if_last(w):
        @pl.when(tab_ref[3, w] > 0)
        def _():
            acc = acc_ref[tab_ref[4, w]]
            emit(tab_ref[0, w], acc[:, :LANES] / acc[:, LANES:])

    @pl.when(first_step)
    def _():
        acc_ref[...] = jnp.zeros(acc_ref.shape, _F32)

    m1[...] = jnp.full(m1.shape, NEG_INF, _F32)
    score(0, s0, m0, a0, m1)

    even, odd = (s0, m0, a0), (s1, m1, a1)

    def trip(t, carry):
        w = ITEMS_PER_TRIP * t
        for i in range(0, ITEMS_PER_TRIP, 2):
            score(w + i + 1, *odd, m0)
            finish(w + i, *even)
            score(w + i + 2, *even, m1)
            finish(w + i + 1, *odd)
        for i in range(ITEMS_PER_TRIP):
            emit_if_last(w + i)
        return carry

    lax.fori_loop(0, trips, trip, 0)


def _attn_scratch(rows, tk):
    s = pltpu.VMEM((rows, tk), _F32)
    r = pltpu.VMEM((rows, LANES), _F32)
    return [s, s, r, r, r, r, pltpu.VMEM((ITEMS_PER_TRIP + 1, rows, V_ONES), _F32)]


def _diff_attn_kernel(tab_ref, q_ref, k_ref, v_ref, lam_ref, o_ref, *scratch, tq, trips,
                      lam_init):
    lane = lax.broadcasted_iota(jnp.int32, (1, QK_W), 1)
    lp = lam_ref[...]
    lam = (jnp.exp(jnp.sum(lp[0:1] * lp[1:2], axis=1, keepdims=True))
           - jnp.exp(jnp.sum(lp[2:3] * lp[3:4], axis=1, keepdims=True)) + lam_init)

    def load_q(qi):
        q = q_ref[0, pl.ds(pl.multiple_of(qi * tq, tq), tq), :]
        zero = jnp.zeros_like(q)
        return jnp.concatenate(
            [jnp.where((lane < DIFF_HEAD_DIM) | (lane >= LANES), q, zero),
             jnp.where(lane >= DIFF_HEAD_DIM, q, zero)], axis=0)

    def emit(qi, o):
        o_ref[0, pl.ds(pl.multiple_of(qi * tq, tq), tq), :] = (
            o[:tq] - lam * o[tq:]).astype(o_ref.dtype)

    first_step = (pl.program_id(0) == 0) & (pl.program_id(1) == 0)
    _flash(tab_ref, load_q, k_ref, v_ref, emit, scratch, trips=trips, first_step=first_step)


def _diff_attn_call(qd, kd, vd, lam_p, *, tq, tk, q_pos0, n_keys, lam_init):
    B, T, _ = qd.shape
    Sk = kd.shape[1]
    tab, trips = _attn_schedule(T, tq, tk, q_pos0, n_keys)
    kern = functools.partial(_diff_attn_kernel, tq=tq, trips=trips, lam_init=lam_init)
    head = lambda rows, c: pl.BlockSpec((1, rows, c), lambda b, h, tab: (b, 0, h))
    return pl.pallas_call(
        kern,
        grid_spec=pltpu.PrefetchScalarGridSpec(
            num_scalar_prefetch=1,
            grid=(B, DIFF_HEADS),
            in_specs=[head(T, QK_W), head(Sk, QK_W), head(Sk, V_ONES),
                      pl.BlockSpec(lam_p.shape, lambda b, h, tab: (0, 0))],
            out_specs=head(T, LANES),
            scratch_shapes=_attn_scratch(2 * tq, tk)),
        out_shape=jax.ShapeDtypeStruct((B, T, DIFF_WIDTH), _BF16),
        compiler_params=pltpu.CompilerParams(
            dimension_semantics=("arbitrary",) * 2, vmem_limit_bytes=VMEM_LIMIT),
        name="diff_attn",
    )(jnp.asarray(tab), qd, kd, vd, lam_p)


def _mla_attn_kernel(tab_ref, q_ref, k_ref, v_ref, o_ref, *scratch, tq, trips):
    def load_q(qi):
        q = q_ref[0, :, pl.ds(pl.multiple_of(qi * tq, tq), tq), :]
        return q.reshape(MLA_HEADS * tq, MLA_QK)

    def emit(qi, o):
        o = o.astype(o_ref.dtype)
        for h in range(MLA_HEADS):
            o_ref[0, pl.ds(pl.multiple_of(qi * tq, tq), tq), h * LANES:(h + 1) * LANES] = (
                o[h * tq:(h + 1) * tq])

    _flash(tab_ref, load_q, k_ref, v_ref, emit, scratch, trips=trips,
           first_step=pl.program_id(0) == 0)


def _mla_attn_call(qm, kk, vv, *, tq, tk, q_pos0, n_keys):
    B, _, T, _ = qm.shape
    Sk = kk.shape[1]
    tab, trips = _attn_schedule(T, tq, tk, q_pos0, n_keys)
    kern = functools.partial(_mla_attn_kernel, tq=tq, trips=trips)
    kv = pl.BlockSpec((1, Sk, MLA_QK), lambda b, tab: (b, 0, 0))
    return pl.pallas_call(
        kern,
        grid_spec=pltpu.PrefetchScalarGridSpec(
            num_scalar_prefetch=1,
            grid=(B,),
            in_specs=[pl.BlockSpec((1, MLA_HEADS, T, MLA_QK), lambda b, tab: (b, 0, 0, 0)),
                      kv, kv],
            out_specs=pl.BlockSpec((1, T, MLA_HEADS * LANES), lambda b, tab: (b, 0, 0)),
            scratch_shapes=_attn_scratch(MLA_HEADS * tq, tk)),
        out_shape=jax.ShapeDtypeStruct((B, T, MLA_HEADS * LANES), _BF16),
        compiler_params=pltpu.CompilerParams(
            dimension_semantics=("arbitrary",), vmem_limit_bytes=VMEM_LIMIT),
        name="mla_attn",
    )(jnp.asarray(tab), qm, kk, vv)


def _ffn_kernel(x_ref, od_ref, om_ref, cprev_ref, gsub_ref, wv_ref, w_od_ref, w_om_ref,
                g_ffn_ref, w_up_ref, w_conv_ref, b_conv_ref, w_down_ref, g_fin_ref,
                y_ref, nconv_ref, carry_ref, gbuf_ref, act_ref, *, tm, lam_init):
    t = pl.program_id(1)

    @pl.when(t == 0)
    def _():
        carry_ref[...] = jnp.zeros(carry_ref.shape, _F32)
        carry_ref[8 - (CONV_W - 1):, :] = cprev_ref[0]

    mix_d = jnp.concatenate(
        [(_rms(od_ref[0, :, h * LANES:(h + 1) * LANES].astype(_F32), gsub_ref[...])
          * (1.0 - lam_init)).astype(_BF16) for h in range(DIFF_HEADS)], axis=1)
    mix_m = jnp.concatenate(
        [_dot(om_ref[0, :, h * LANES:(h + 1) * LANES], wv_ref[h]).astype(_BF16)
         for h in range(MLA_HEADS)], axis=1)
    x1 = x_ref[0] + _dot(mix_d, w_od_ref[...]) + _dot(mix_m, w_om_ref[...])
    h2 = _rms(x1, g_ffn_ref[...]).astype(_BF16)
    for c in range(D_FF // FF_CHUNK):
        cs = slice(c * FF_CHUNK, (c + 1) * FF_CHUNK)
        u = _dot(h2, w_up_ref[:, cs])
        g = _dot(h2, w_up_ref[:, D_FF + c * FF_CHUNK:D_FF + (c + 1) * FF_CHUNK])
        gbuf_ref[0:8, :] = carry_ref[:, cs]
        gbuf_ref[8:, :] = g
        conv = (b_conv_ref[:, cs] + gbuf_ref[6:6 + tm, :] * w_conv_ref[0:1, cs]
                + gbuf_ref[7:7 + tm, :] * w_conv_ref[1:2, cs] + g * w_conv_ref[2:3, cs])
        half = 0.5 * conv
        act_ref[:, cs] = ((half + half * jnp.tanh(half)) * u).astype(_BF16)
        carry_ref[:, cs] = g[tm - 8:, :]
    nconv_ref[0] = carry_ref[8 - (CONV_W - 1):, :]
    y_ref[0] = _rms(x1 + _dot(act_ref[...], w_down_ref[...]), g_fin_ref[...])


def _ffn_call(x, o_d, o_m, conv_prev, g_sub, wv, w_od, w_om, g_ffn, w_up, w_conv, b_conv,
              w_down, g_fin, *, tm, lam_init):
    B, T, _ = x.shape
    row = lambda c: pl.BlockSpec((1, tm, c), lambda b, t: (b, t, 0))
    full = lambda a: pl.BlockSpec(a.shape, lambda b, t: (0,) * a.ndim,
                                  pipeline_mode=pl.Buffered(1))
    state = pl.BlockSpec((1, CONV_W - 1, D_FF), lambda b, t: (b, 0, 0))
    weights = (g_sub, wv, w_od, w_om, g_ffn, w_up, w_conv, b_conv, w_down, g_fin)
    return pl.pallas_call(
        functools.partial(_ffn_kernel, tm=tm, lam_init=lam_init),
        grid=(B, T // tm),
        in_specs=[row(D_MODEL), row(DIFF_WIDTH), row(MLA_WIDTH), state]
                 + [full(w) for w in weights],
        out_specs=(row(D_MODEL), state),
        out_shape=(jax.ShapeDtypeStruct((B, T, D_MODEL), _F32),
                   jax.ShapeDtypeStruct((B, CONV_W - 1, D_FF), _F32)),
        scratch_shapes=[pltpu.VMEM((8, D_FF), _F32), pltpu.VMEM((tm + 8, FF_CHUNK), _F32),
                        pltpu.VMEM((tm, D_FF), _BF16)],
        compiler_params=pltpu.CompilerParams(
            dimension_semantics=("arbitrary", "arbitrary"), vmem_limit_bytes=VMEM_LIMIT),
        name="ffn",
    )(x, o_d, o_m, conv_prev, *weights)


def _rope_tables(pos):
    half = DIFF_HEAD_DIM // 2
    inv = ROPE_THETA ** (-jnp.arange(half, dtype=_F32) * (2.0 / DIFF_HEAD_DIM))
    ang = pos.astype(_F32)[:, None] * inv[None, :]
    cos, sin, zero = jnp.cos(ang), jnp.sin(ang), jnp.zeros_like(ang)
    cos2 = jnp.concatenate([cos, cos, cos, cos], axis=1)
    sin_lo = jnp.concatenate([-sin, zero, -sin, zero], axis=1)
    sin_hi = jnp.concatenate([zero, sin, zero, sin], axis=1)
    return cos2, sin_lo, sin_hi


def _tiles(T, n_keys):
    if T >= 512:
        return dict(tm=512, diff=(512, 512), mla=(256, 512))
    tk = -(-n_keys // LANES) * LANES
    return dict(tm=T, diff=(T, tk), mla=(T, tk))


def _layer(x, pos, past, lam_init, wl, g_final):
    (g_attn, w_in_p, lam_p, g_sub, g_qa, w_qn, w_qr, wkt, wv, g_kva, w_od, w_om, g_ffn, w_up,
     w_conv, b_conv, w_down) = wl
    B, T, _ = x.shape
    n_past = 0 if past is None else past[0].shape[1]
    n_keys, q_pos0 = n_past + T, n_past
    tl = _tiles(T, n_keys)
    tabs = _rope_tables(pos)
    qd, kd32, kd16, vd32, vd16, ckv32, kr32, kmla, vmla, qmla = _proj_call(
        x, tabs, g_attn, w_in_p, g_qa, w_qn, w_qr, wkt, g_kva, tm=tl["tm"], q_pos0=q_pos0)

    if past is None:
        kd_all, vd_all, kmla_all, vmla_all = kd16, vd16, kmla, vmla
        conv_prev = jnp.zeros((B, CONV_W - 1, D_FF), _F32)
    else:
        p_dk, p_dv, p_ckv, p_kr, conv_prev = past
        n_pad = tl["diff"][1] - n_keys
        lane = jnp.arange(LANES)[None, :]
        k_cols = jnp.where(((jnp.arange(n_past) // CHUNK)[:, None] > lane) & (lane < MASK_COLS),
                           NEG_INF, 0.0).astype(_BF16)
        k_cols = jnp.broadcast_to(k_cols, (B, n_past, LANES))
        pad_cols = jnp.where(lane < MASK_COLS, NEG_INF, 0.0).astype(_BF16)
        one = jnp.ones((B, n_past, LANES), _BF16)
        p_dk = p_dk.reshape(B, n_past, DIFF_WIDTH).astype(_BF16)
        p_dv = p_dv.reshape(B, n_past, DIFF_WIDTH).astype(_BF16)
        heads = lambda a: [a[:, :, h * LANES:(h + 1) * LANES] for h in range(DIFF_HEADS)]

        def cat(old, new, pad_row):
            pad = jnp.broadcast_to(pad_row, (B, n_pad, new.shape[-1]))
            return jnp.concatenate([jnp.concatenate(old, axis=-1), new, pad], axis=1)

        zero = lambda w: jnp.zeros((1, 1, w), _BF16)
        kd_all = cat([p for a in heads(p_dk) for p in (a, k_cols)], kd16,
                     jnp.tile(jnp.concatenate([zero(LANES)[0], pad_cols], axis=-1), DIFF_HEADS))
        vd_all = cat([p for a in heads(p_dv) for p in (a, one)], vd16, zero(DIFF_HEADS * V_ONES))
        kmla_all = cat([p_ckv.astype(_BF16), p_kr.astype(_BF16), k_cols[:, :, :MASK_COLS]], kmla,
                       jnp.concatenate([zero(LANES + MLA_ROPE)[0], pad_cols[:, :MASK_COLS]], -1))
        vmla_all = cat([p_ckv.astype(_BF16), one], vmla, zero(V_ONES))

    o_d = _diff_attn_call(qd, kd_all, vd_all, lam_p, tq=tl["diff"][0], tk=tl["diff"][1],
                          q_pos0=q_pos0, n_keys=n_keys, lam_init=lam_init)
    o_m = _mla_attn_call(qmla, kmla_all, vmla_all, tq=tl["mla"][0], tk=tl["mla"][1],
                         q_pos0=q_pos0, n_keys=n_keys)
    y, new_conv = _ffn_call(x, o_d, o_m, conv_prev, g_sub, wv, w_od, w_om, g_ffn, w_up, w_conv,
                            b_conv, w_down, g_final, tm=tl["tm"], lam_init=lam_init)
    state = (kd32.reshape(1, B, T, DIFF_HEADS, 2, DIFF_HEAD_DIM),
             vd32.reshape(1, B, T, DIFF_HEADS, 2 * DIFF_HEAD_DIM),
             ckv32[None], kr32[None], new_conv[None])
    return y, state


def kernel(x_prompt, x_sample, cache_diff_k, cache_diff_v, cache_mla_ckv, cache_mla_krope,
           state_conv, g_attn, w_in, lambda_q1, lambda_k1, lambda_q2, lambda_k2, g_diff_sub,
           g_q_lora, w_q_b, g_kv_lora, w_kv_b, w_out, g_ffn, w_up, w_conv, b_conv, w_down,
           g_final):
    assert g_attn.shape[0] == 1, "single-layer model"
    S = x_prompt.shape[1]
    T = x_sample.shape[1]
    P = cache_diff_k.shape[2]
    lam_init = 0.8 - 0.6 * math.exp(-0.3 * 0)

    w_in_p = jnp.concatenate([w_in[0], w_in[0][:, OFF_KR:IN_COLS]], axis=1).astype(_BF16)
    wq = w_q_b[0].reshape(MLA_Q_LORA, MLA_HEADS, MLA_NOPE + MLA_ROPE)
    w_qn = wq[:, :, :MLA_NOPE].reshape(MLA_Q_LORA, MLA_HEADS * MLA_NOPE).astype(_BF16)
    w_qr = wq[:, :, MLA_NOPE:].reshape(MLA_Q_LORA, MLA_HEADS * MLA_ROPE).astype(_BF16)
    wkv = w_kv_b[0].reshape(MLA_KV_LORA, MLA_HEADS, MLA_NOPE + MLA_V)
    wkt = jnp.transpose(wkv[:, :, :MLA_NOPE], (1, 2, 0)).astype(_BF16)
    wv = jnp.transpose(wkv[:, :, MLA_NOPE:], (1, 0, 2)).astype(_BF16)
    lam_p = jnp.concatenate([lambda_q1, lambda_k1, lambda_q2, lambda_k2], axis=0)
    wl = (g_attn, w_in_p, lam_p, g_diff_sub, g_q_lora, w_qn, w_qr, wkt, wv, g_kv_lora,
          w_out[0][:DIFF_WIDTH].astype(_BF16), w_out[0][DIFF_WIDTH:].astype(_BF16), g_ffn,
          w_up[0].astype(_BF16), w_conv[0], b_conv, w_down[0].astype(_BF16))
    g_fin = g_final[None]

    pos_p = jnp.arange(S, dtype=jnp.int32)
    pos_s = P + jnp.arange(T, dtype=jnp.int32)
    y_p, st_p = _layer(x_prompt, pos_p, None, lam_init, wl, g_fin)
    past = (cache_diff_k[0], cache_diff_v[0], cache_mla_ckv[0], cache_mla_krope[0], state_conv[0])
    y_s, st_s = _layer(x_sample, pos_s, past, lam_init, wl, g_fin)
    return (y_p, y_s) + st_p + st_s
```

```python
import functools
import math

import jax
import jax.numpy as jnp
import numpy as np
from jax import lax
from jax.experimental import pallas as pl
from jax.experimental.pallas import tpu as pltpu

D_MODEL = 1024
CHUNK = 64
ROPE_THETA = 10000.0
NORM_EPS = 1e-6
NEG_INF = -1e30

DIFF_HEADS = 4
DIFF_HEAD_DIM = 64
DIFF_WIDTH = DIFF_HEADS * 2 * DIFF_HEAD_DIM
MLA_HEADS = 4
MLA_Q_LORA = 256
MLA_KV_LORA = 128
MLA_NOPE = 128
MLA_ROPE = 64
MLA_V = 128
MLA_WIDTH = MLA_HEADS * MLA_V
MLA_SCALE = (MLA_NOPE + MLA_ROPE) ** -0.5
OFF_DQ = 0
OFF_DK = OFF_DQ + DIFF_WIDTH
OFF_DV = OFF_DK + DIFF_WIDTH
OFF_CQ = OFF_DV + DIFF_WIDTH
OFF_CKV = OFF_CQ + MLA_Q_LORA
OFF_KR = OFF_CKV + MLA_KV_LORA
IN_COLS = OFF_KR + MLA_ROPE
IN_COLS_PAD = IN_COLS + MLA_ROPE
D_FF = 2816
CONV_W = 3

LANES = 128
MASK_COLS = 64
QK_W = 2 * LANES
MLA_QK = 2 * LANES
V_ONES = 2 * LANES
FF_CHUNK = 256
TRIP_SIZES = (2, 4, 6)
LOG2E = math.log2(math.e)
VMEM_LIMIT = 56 * 1024 * 1024

_BF16 = jnp.bfloat16
_F32 = jnp.float32


def _dot(a, b):
    return jnp.dot(a, b, preferred_element_type=_F32)


def _dot_t(a, b):
    return lax.dot_general(a, b, (((1,), (1,)), ((), ())), preferred_element_type=_F32)


def _rms(xf, g):
    return xf * lax.rsqrt(jnp.mean(xf * xf, axis=-1, keepdims=True) + NORM_EPS) * g


def _rope_slab(xs, cos2, sin_lo, sin_hi):
    return (xs * cos2 + pltpu.roll(xs, LANES - 32, axis=1) * sin_lo
            + pltpu.roll(xs, 32, axis=1) * sin_hi)


def _chunk_mask_cols(pos):
    lane = lax.broadcasted_iota(jnp.int32, (1, LANES), 1)
    chunk = lax.shift_right_logical(pos, CHUNK.bit_length() - 1)
    q_cols = jnp.where(lane == chunk, 1.0, 0.0)
    k_cols = jnp.where((chunk > lane) & (lane < MASK_COLS), NEG_INF, 0.0)
    return q_cols, k_cols


def _proj_kernel(x_ref, cos_ref, slo_ref, shi_ref, g_attn_ref, w_in_ref, g_qa_ref, w_qn_ref,
                 w_qr_ref, wkt_ref, g_kva_ref,
                 qd_ref, kd32_ref, kd16_ref, vd32_ref, vd16_ref, ckv32_ref, kr32_ref,
                 kmla_ref, vmla_ref, qmla_ref, *, tm, q_pos0):
    x = x_ref[0]
    h = _rms(x, g_attn_ref[...]).astype(_BF16)
    cos2, slo, shi = cos_ref[...], slo_ref[...], shi_ref[...]
    rope = functools.partial(_rope_slab, cos2=cos2, sin_lo=slo, sin_hi=shi)
    ones = jnp.ones((tm, LANES), _BF16)
    pos = q_pos0 + pl.program_id(1) * tm + lax.broadcasted_iota(jnp.int32, (tm, 1), 0)
    q_cols, k_cols = _chunk_mask_cols(pos)
    lane = lax.broadcasted_iota(jnp.int32, (1, LANES), 1)

    lat = _dot(h, w_in_ref[:, OFF_CQ:IN_COLS_PAD])
    c_q = _rms(lat[:, :MLA_Q_LORA], g_qa_ref[...]).astype(_BF16)
    c_kv = _rms(lat[:, MLA_Q_LORA:OFF_KR - OFF_CQ], g_kva_ref[...])
    k_r2 = rope(lat[:, OFF_KR - OFF_CQ:])

    proj = _dot(h, w_in_ref[:, :OFF_CQ])
    q_scale = DIFF_HEAD_DIM ** -0.5 * LOG2E
    for s in range(DIFF_HEADS):
        cs = slice(s * LANES, (s + 1) * LANES)
        qd_ref[0, :, s * QK_W:s * QK_W + LANES] = (
            rope(proj[:, OFF_DQ + s * LANES:OFF_DQ + (s + 1) * LANES]) * q_scale).astype(_BF16)
        qd_ref[0, :, s * QK_W + LANES:(s + 1) * QK_W] = q_cols.astype(_BF16)
        k = rope(proj[:, OFF_DK + s * LANES:OFF_DK + (s + 1) * LANES])
        kd32_ref[0, :, cs] = k
        kd16_ref[0, :, s * QK_W:s * QK_W + LANES] = k.astype(_BF16)
        kd16_ref[0, :, s * QK_W + LANES:(s + 1) * QK_W] = k_cols.astype(_BF16)
        v = proj[:, OFF_DV + s * LANES:OFF_DV + (s + 1) * LANES]
        vd32_ref[0, :, s, :] = v
        vd16_ref[0, :, s * V_ONES:s * V_ONES + LANES] = v.astype(_BF16)
        vd16_ref[0, :, s * V_ONES + LANES:(s + 1) * V_ONES] = ones

    ckv32_ref[0] = c_kv
    kr32_ref[0] = k_r2[:, :MLA_ROPE]
    kmla_ref[0, :, :LANES] = c_kv.astype(_BF16)
    kmla_ref[0, :, LANES:] = jnp.where(lane < MLA_ROPE, k_r2,
                                       pltpu.roll(k_cols, MLA_ROPE, axis=1)).astype(_BF16)
    vmla_ref[0, :, :LANES] = c_kv.astype(_BF16)
    vmla_ref[0, :, LANES:] = ones

    m_scale = MLA_SCALE * LOG2E
    q_nope = _dot(c_q, w_qn_ref[...]).astype(_BF16)
    q_rope = _dot(c_q, w_qr_ref[...])
    q_cols_hi = pltpu.roll(q_cols, MLA_ROPE, axis=1)
    for hd in range(MLA_HEADS):
        q_lat = _dot(q_nope[:, hd * MLA_NOPE:(hd + 1) * MLA_NOPE], wkt_ref[hd])
        qmla_ref[0, hd, :, :LANES] = (q_lat * m_scale).astype(_BF16)
        if hd % 2 == 0:
            slab = rope(q_rope[:, (hd // 2) * LANES:(hd // 2 + 1) * LANES]) * m_scale
        else:
            slab = pltpu.roll(slab, MLA_ROPE, axis=1)
        qmla_ref[0, hd, :, LANES:] = jnp.where(lane < MLA_ROPE, slab, q_cols_hi).astype(_BF16)


def _proj_call(x, tabs, g_attn, w_in_p, g_qa, w_qn, w_qr, wkt, g_kva, *, tm, q_pos0):
    B, T, _ = x.shape
    nt = T // tm
    assert -(-(q_pos0 + T) // CHUNK) <= MASK_COLS, "chunk one-hot must fit the mask columns"
    row = lambda *c: pl.BlockSpec((1, tm) + c, lambda b, t: (b, t) + (0,) * len(c))
    tab = pl.BlockSpec((tm, LANES), lambda b, t: (t, 0))
    full = lambda a: pl.BlockSpec(a.shape, lambda b, t: (0,) * a.ndim)
    outs = (
        ((DIFF_HEADS * QK_W,), _BF16),
        ((DIFF_WIDTH,), _F32),
        ((DIFF_HEADS * QK_W,), _BF16),
        ((DIFF_HEADS, LANES), _F32),
        ((DIFF_HEADS * V_ONES,), _BF16),
        ((MLA_KV_LORA,), _F32),
        ((MLA_ROPE,), _F32),
        ((MLA_QK,), _BF16),
        ((V_ONES,), _BF16),
    )
    out_specs = tuple(row(*c) for c, _ in outs) + (
        pl.BlockSpec((1, MLA_HEADS, tm, MLA_QK), lambda b, t: (b, 0, t, 0)),)
    out_shape = tuple(jax.ShapeDtypeStruct((B, T) + c, d) for c, d in outs) + (
        jax.ShapeDtypeStruct((B, MLA_HEADS, T, MLA_QK), _BF16),)
    return pl.pallas_call(
        functools.partial(_proj_kernel, tm=tm, q_pos0=q_pos0),
        grid=(B, nt),
        in_specs=[row(D_MODEL), tab, tab, tab, full(g_attn), full(w_in_p), full(g_qa),
                  full(w_qn), full(w_qr), full(wkt), full(g_kva)],
        out_specs=out_specs,
        out_shape=out_shape,
        compiler_params=pltpu.CompilerParams(
            dimension_semantics=("arbitrary", "arbitrary"), vmem_limit_bytes=VMEM_LIMIT),
        name="proj",
    )(x, *tabs, g_attn, w_in_p, g_qa, w_qn, w_qr, wkt, g_kva)


def _attn_schedule(T, tq, tk, q_pos0, n_keys):
    pairs = []
    for qi in range(T // tq):
        last_pos = q_pos0 + (qi + 1) * tq - 1
        n_kv = -(-min((last_pos // CHUNK + 1) * CHUNK, n_keys) // tk)
        pairs += [(qi, kj, n_kv) for kj in range(n_kv)]
    per_trip = min(TRIP_SIZES, key=lambda n: (-len(pairs) % n, -n))
    items = [(qi, kj, int(kj == 0), int(kj == n_kv - 1), qi % per_trip) for qi, kj, n_kv in pairs]
    trips = -(-len(items) // per_trip)
    items += [(0, 0, 1, 0, per_trip)] * (per_trip * trips + 1 - len(items))
    return np.asarray(items, np.int32).T.copy(), trips, per_trip


def _flash(tab_ref, load_q, k_ref, v_ref, emit, scratch, *, trips, first_step):
    (s0, s1, m0, m1, a0, a1, acc_ref) = scratch
    tk = s0.shape[1]

    def score(w, s_ref, m_ref, a_ref, m_prev_ref):
        k0 = pl.multiple_of(tab_ref[1, w] * tk, tk)
        s = _dot_t(load_q(tab_ref[0, w]), k_ref[0, pl.ds(k0, tk), :])
        m_prev = jnp.where(tab_ref[2, w] > 0, NEG_INF, m_prev_ref[...])
        m_new = jnp.maximum(m_prev, jnp.max(s, axis=1, keepdims=True))
        s_ref[...] = s
        a_ref[...] = jnp.exp2(m_prev - m_new)
        m_ref[...] = m_new

    def finish(w, s_ref, m_ref, a_ref):
        k0 = pl.multiple_of(tab_ref[1, w] * tk, tk)
        slot = tab_ref[4, w]
        p = jnp.exp2(s_ref[...] - jnp.tile(m_ref[...], (1, tk // LANES)))
        pv = _dot(p.astype(_BF16), v_ref[0, pl.ds(k0, tk), :])
        acc_ref[slot] = jnp.tile(a_ref[...], (1, V_ONES // LANES)) * acc_ref[slot] + pv

    def emit_if_last(w):
        @pl.when(tab_ref[3, w] > 0)
        def _():
            acc = acc_ref[tab_ref[4, w]]
            emit(tab_ref[0, w], acc[:, :LANES] / acc[:, LANES:])

    @pl.when(first_step)
    def _():
        acc_ref[...] = jnp.zeros(acc_ref.shape, _F32)

    m1[...] = jnp.full(m1.shape, NEG_INF, _F32)
    score(0, s0, m0, a0, m1)

    even, odd = (s0, m0, a0), (s1, m1, a1)
    per_trip = acc_ref.shape[0] - 1

    def trip(t, carry):
        w = per_trip * t
        for i in range(0, per_trip, 2):
            score(w + i + 1, *odd, m0)
            finish(w + i, *even)
            score(w + i + 2, *even, m1)
            finish(w + i + 1, *odd)
        for i in range(per_trip):
            emit_if_last(w + i)
        return carry

    lax.fori_loop(0, trips, trip, 0)


def _attn_scratch(rows, tk, per_trip):
    s = pltpu.VMEM((rows, tk), _F32)
    r = pltpu.VMEM((rows, LANES), _F32)
    return [s, s, r, r, r, r, pltpu.VMEM((per_trip + 1, rows, V_ONES), _F32)]


def _diff_attn_kernel(tab_ref, q_ref, k_ref, v_ref, lam_ref, o_ref, *scratch, tq, trips,
                      lam_init):
    lane = lax.broadcasted_iota(jnp.int32, (1, QK_W), 1)
    lp = lam_ref[...]
    lam = (jnp.exp(jnp.sum(lp[0:1] * lp[1:2], axis=1, keepdims=True))
           - jnp.exp(jnp.sum(lp[2:3] * lp[3:4], axis=1, keepdims=True)) + lam_init)

    def load_q(qi):
        q = q_ref[0, pl.ds(pl.multiple_of(qi * tq, tq), tq), :]
        zero = jnp.zeros_like(q)
        return jnp.concatenate(
            [jnp.where((lane < DIFF_HEAD_DIM) | (lane >= LANES), q, zero),
             jnp.where(lane >= DIFF_HEAD_DIM, q, zero)], axis=0)

    def emit(qi, o):
        o_ref[0, pl.ds(pl.multiple_of(qi * tq, tq), tq), :] = (
            o[:tq] - lam * o[tq:]).astype(o_ref.dtype)

    first_step = (pl.program_id(0) == 0) & (pl.program_id(1) == 0)
    _flash(tab_ref, load_q, k_ref, v_ref, emit, scratch, trips=trips, first_step=first_step)


def _diff_attn_call(qd, kd, vd, lam_p, *, tq, tk, q_pos0, n_keys, lam_init):
    B, T, _ = qd.shape
    Sk = kd.shape[1]
    tab, trips, per_trip = _attn_schedule(T, tq, tk, q_pos0, n_keys)
    kern = functools.partial(_diff_attn_kernel, tq=tq, trips=trips, lam_init=lam_init)
    head = lambda rows, c: pl.BlockSpec((1, rows, c), lambda b, h, tab: (b, 0, h))
    return pl.pallas_call(
        kern,
        grid_spec=pltpu.PrefetchScalarGridSpec(
            num_scalar_prefetch=1,
            grid=(B, DIFF_HEADS),
            in_specs=[head(T, QK_W), head(Sk, QK_W), head(Sk, V_ONES),
                      pl.BlockSpec(lam_p.shape, lambda b, h, tab: (0, 0))],
            out_specs=head(T, LANES),
            scratch_shapes=_attn_scratch(2 * tq, tk, per_trip)),
        out_shape=jax.ShapeDtypeStruct((B, T, DIFF_WIDTH), _BF16),
        compiler_params=pltpu.CompilerParams(
            dimension_semantics=("arbitrary",) * 2, vmem_limit_bytes=VMEM_LIMIT),
        name="diff_attn",
    )(jnp.asarray(tab), qd, kd, vd, lam_p)


def _mla_attn_kernel(tab_ref, q_ref, k_ref, v_ref, o_ref, *scratch, tq, trips):
    def load_q(qi):
        q = q_ref[0, :, pl.ds(pl.multiple_of(qi * tq, tq), tq), :]
        return q.reshape(MLA_HEADS * tq, MLA_QK)

    def emit(qi, o):
        o = o.astype(o_ref.dtype)
        for h in range(MLA_HEADS):
            o_ref[0, pl.ds(pl.multiple_of(qi * tq, tq), tq), h * LANES:(h + 1) * LANES] = (
                o[h * tq:(h + 1) * tq])

    _flash(tab_ref, load_q, k_ref, v_ref, emit, scratch, trips=trips,
           first_step=pl.program_id(0) == 0)


def _mla_attn_call(qm, kk, vv, *, tq, tk, q_pos0, n_keys):
    B, _, T, _ = qm.shape
    Sk = kk.shape[1]
    tab, trips, per_trip = _attn_schedule(T, tq, tk, q_pos0, n_keys)
    kern = functools.partial(_mla_attn_kernel, tq=tq, trips=trips)
    kv = pl.BlockSpec((1, Sk, MLA_QK), lambda b, tab: (b, 0, 0))
    return pl.pallas_call(
        kern,
        grid_spec=pltpu.PrefetchScalarGridSpec(
            num_scalar_prefetch=1,
            grid=(B,),
            in_specs=[pl.BlockSpec((1, MLA_HEADS, T, MLA_QK), lambda b, tab: (b, 0, 0, 0)),
                      kv, kv],
            out_specs=pl.BlockSpec((1, T, MLA_HEADS * LANES), lambda b, tab: (b, 0, 0)),
            scratch_shapes=_attn_scratch(MLA_HEADS * tq, tk, per_trip)),
        out_shape=jax.ShapeDtypeStruct((B, T, MLA_HEADS * LANES), _BF16),
        compiler_params=pltpu.CompilerParams(
            dimension_semantics=("arbitrary",), vmem_limit_bytes=VMEM_LIMIT),
        name="mla_attn",
    )(jnp.asarray(tab), qm, kk, vv)


def _ffn_kernel(x_ref, od_ref, om_ref, cprev_ref, gsub_ref, wv_ref, w_od_ref, w_om_ref,
                g_ffn_ref, w_up_ref, w_conv_ref, b_conv_ref, w_down_ref, g_fin_ref,
                y_ref, nconv_ref, carry_ref, gbuf_ref, act_ref, *, tm, lam_init):
    t = pl.program_id(1)

    @pl.when(t == 0)
    def _():
        carry_ref[...] = jnp.zeros(carry_ref.shape, _F32)
        carry_ref[8 - (CONV_W - 1):, :] = cprev_ref[0]

    mix_d = jnp.concatenate(
        [(_rms(od_ref[0, :, h * LANES:(h + 1) * LANES].astype(_F32), gsub_ref[...])
          * (1.0 - lam_init)).astype(_BF16) for h in range(DIFF_HEADS)], axis=1)
    mix_m = jnp.concatenate(
        [_dot(om_ref[0, :, h * LANES:(h + 1) * LANES], wv_ref[h]).astype(_BF16)
         for h in range(MLA_HEADS)], axis=1)
    x1 = x_ref[0] + _dot(mix_d, w_od_ref[...]) + _dot(mix_m, w_om_ref[...])
    h2 = _rms(x1, g_ffn_ref[...]).astype(_BF16)
    for c in range(D_FF // FF_CHUNK):
        cs = slice(c * FF_CHUNK, (c + 1) * FF_CHUNK)
        u = _dot(h2, w_up_ref[:, cs])
        g = _dot(h2, w_up_ref[:, D_FF + c * FF_CHUNK:D_FF + (c + 1) * FF_CHUNK])
        gbuf_ref[0:8, :] = carry_ref[:, cs]
        gbuf_ref[8:, :] = g
        conv = (b_conv_ref[:, cs] + gbuf_ref[6:6 + tm, :] * w_conv_ref[0:1, cs]
                + gbuf_ref[7:7 + tm, :] * w_conv_ref[1:2, cs] + g * w_conv_ref[2:3, cs])
        half = 0.5 * conv
        act_ref[:, cs] = ((half + half * jnp.tanh(half)) * u).astype(_BF16)
        carry_ref[:, cs] = g[tm - 8:, :]
    nconv_ref[0] = carry_ref[8 - (CONV_W - 1):, :]
    y_ref[0] = _rms(x1 + _dot(act_ref[...], w_down_ref[...]), g_fin_ref[...])


def _ffn_call(x, o_d, o_m, conv_prev, g_sub, wv, w_od, w_om, g_ffn, w_up, w_conv, b_conv,
              w_down, g_fin, *, tm, lam_init):
    B, T, _ = x.shape
    row = lambda c: pl.BlockSpec((1, tm, c), lambda b, t: (b, t, 0))
    full = lambda a: pl.BlockSpec(a.shape, lambda b, t: (0,) * a.ndim,
                                  pipeline_mode=pl.Buffered(1))
    state = pl.BlockSpec((1, CONV_W - 1, D_FF), lambda b, t: (b, 0, 0))
    weights = (g_sub, wv, w_od, w_om, g_ffn, w_up, w_conv, b_conv, w_down, g_fin)
    return pl.pallas_call(
        functools.partial(_ffn_kernel, tm=tm, lam_init=lam_init),
        grid=(B, T // tm),
        in_specs=[row(D_MODEL), row(DIFF_WIDTH), row(MLA_WIDTH), state]
                 + [full(w) for w in weights],
        out_specs=(row(D_MODEL), state),
        out_shape=(jax.ShapeDtypeStruct((B, T, D_MODEL), _F32),
                   jax.ShapeDtypeStruct((B, CONV_W - 1, D_FF), _F32)),
        scratch_shapes=[pltpu.VMEM((8, D_FF), _F32), pltpu.VMEM((tm + 8, FF_CHUNK), _F32),
                        pltpu.VMEM((tm, D_FF), _BF16)],
        compiler_params=pltpu.CompilerParams(
            dimension_semantics=("arbitrary", "arbitrary"), vmem_limit_bytes=VMEM_LIMIT),
        name="ffn",
    )(x, o_d, o_m, conv_prev, *weights)


def _rope_tables(pos):
    half = DIFF_HEAD_DIM // 2
    inv = ROPE_THETA ** (-jnp.arange(half, dtype=_F32) * (2.0 / DIFF_HEAD_DIM))
    ang = pos.astype(_F32)[:, None] * inv[None, :]
    cos, sin, zero = jnp.cos(ang), jnp.sin(ang), jnp.zeros_like(ang)
    cos2 = jnp.concatenate([cos, cos, cos, cos], axis=1)
    sin_lo = jnp.concatenate([-sin, zero, -sin, zero], axis=1)
    sin_hi = jnp.concatenate([zero, sin, zero, sin], axis=1)
    return cos2, sin_lo, sin_hi


def _tiles(T, n_keys):
    if T >= 512:
        return dict(tm=512, diff=(512, 512), mla=(256, 512))
    tk = -(-n_keys // LANES) * LANES
    return dict(tm=T, diff=(T, tk), mla=(T, tk))


def _layer(x, pos, past, lam_init, wl, g_final):
    (g_attn, w_in_p, lam_p, g_sub, g_qa, w_qn, w_qr, wkt, wv, g_kva, w_od, w_om, g_ffn, w_up,
     w_conv, b_conv, w_down) = wl
    B, T, _ = x.shape
    n_past = 0 if past is None else past[0].shape[1]
    n_keys, q_pos0 = n_past + T, n_past
    tl = _tiles(T, n_keys)
    tabs = _rope_tables(pos)
    qd, kd32, kd16, vd32, vd16, ckv32, kr32, kmla, vmla, qmla = _proj_call(
        x, tabs, g_attn, w_in_p, g_qa, w_qn, w_qr, wkt, g_kva, tm=tl["tm"], q_pos0=q_pos0)

    if past is None:
        kd_all, vd_all, kmla_all, vmla_all = kd16, vd16, kmla, vmla
        conv_prev = jnp.zeros((B, CONV_W - 1, D_FF), _F32)
    else:
        p_dk, p_dv, p_ckv, p_kr, conv_prev = past
        n_pad = tl["diff"][1] - n_keys
        lane = jnp.arange(LANES)[None, :]
        k_cols = jnp.where(((jnp.arange(n_past) // CHUNK)[:, None] > lane) & (lane < MASK_COLS),
                           NEG_INF, 0.0).astype(_BF16)
        k_cols = jnp.broadcast_to(k_cols, (B, n_past, LANES))
        pad_cols = jnp.where(lane < MASK_COLS, NEG_INF, 0.0).astype(_BF16)
        one = jnp.ones((B, n_past, LANES), _BF16)
        p_dk = p_dk.reshape(B, n_past, DIFF_WIDTH).astype(_BF16)
        p_dv = p_dv.reshape(B, n_past, DIFF_WIDTH).astype(_BF16)
        heads = lambda a: [a[:, :, h * LANES:(h + 1) * LANES] for h in range(DIFF_HEADS)]

        def cat(old, new, pad_row):
            pad = jnp.broadcast_to(pad_row, (B, n_pad, new.shape[-1]))
            return jnp.concatenate([jnp.concatenate(old, axis=-1), new, pad], axis=1)

        zero = lambda w: jnp.zeros((1, 1, w), _BF16)
        kd_all = cat([p for a in heads(p_dk) for p in (a, k_cols)], kd16,
                     jnp.tile(jnp.concatenate([zero(LANES)[0], pad_cols], axis=-1), DIFF_HEADS))
        vd_all = cat([p for a in heads(p_dv) for p in (a, one)], vd16, zero(DIFF_HEADS * V_ONES))
        kmla_all = cat([p_ckv.astype(_BF16), p_kr.astype(_BF16), k_cols[:, :, :MASK_COLS]], kmla,
                       jnp.concatenate([zero(LANES + MLA_ROPE)[0], pad_cols[:, :MASK_COLS]], -1))
        vmla_all = cat([p_ckv.astype(_BF16), one], vmla, zero(V_ONES))

    o_d = _diff_attn_call(qd, kd_all, vd_all, lam_p, tq=tl["diff"][0], tk=tl["diff"][1],
                          q_pos0=q_pos0, n_keys=n_keys, lam_init=lam_init)
    o_m = _mla_attn_call(qmla, kmla_all, vmla_all, tq=tl["mla"][0], tk=tl["mla"][1],
                         q_pos0=q_pos0, n_keys=n_keys)
    y, new_conv = _ffn_call(x, o_d, o_m, conv_prev, g_sub, wv, w_od, w_om, g_ffn, w_up, w_conv,
                            b_conv, w_down, g_final, tm=tl["tm"], lam_init=lam_init)
    state = (kd32.reshape(1, B, T, DIFF_HEADS, 2, DIFF_HEAD_DIM),
             vd32[None],
             ckv32[None], kr32[None], new_conv[None])
    return y, state


def kernel(x_prompt, x_sample, cache_diff_k, cache_diff_v, cache_mla_ckv, cache_mla_krope,
           state_conv, g_attn, w_in, lambda_q1, lambda_k1, lambda_q2, lambda_k2, g_diff_sub,
           g_q_lora, w_q_b, g_kv_lora, w_kv_b, w_out, g_ffn, w_up, w_conv, b_conv, w_down,
           g_final):
    assert g_attn.shape[0] == 1, "single-layer model"
    S = x_prompt.shape[1]
    T = x_sample.shape[1]
    P = cache_diff_k.shape[2]
    lam_init = 0.8 - 0.6 * math.exp(-0.3 * 0)

    w_in_p = jnp.concatenate([w_in[0], w_in[0][:, OFF_KR:IN_COLS]], axis=1).astype(_BF16)
    wq = w_q_b[0].reshape(MLA_Q_LORA, MLA_HEADS, MLA_NOPE + MLA_ROPE)
    w_qn = wq[:, :, :MLA_NOPE].reshape(MLA_Q_LORA, MLA_HEADS * MLA_NOPE).astype(_BF16)
    w_qr = wq[:, :, MLA_NOPE:].reshape(MLA_Q_LORA, MLA_HEADS * MLA_ROPE).astype(_BF16)
    wkv = w_kv_b[0].reshape(MLA_KV_LORA, MLA_HEADS, MLA_NOPE + MLA_V)
    wkt = jnp.transpose(wkv[:, :, :MLA_NOPE], (1, 2, 0)).astype(_BF16)
    wv = jnp.transpose(wkv[:, :, MLA_NOPE:], (1, 0, 2)).astype(_BF16)
    lam_p = jnp.concatenate([lambda_q1, lambda_k1, lambda_q2, lambda_k2], axis=0)
    wl = (g_attn, w_in_p, lam_p, g_diff_sub, g_q_lora, w_qn, w_qr, wkt, wv, g_kv_lora,
          w_out[0][:DIFF_WIDTH].astype(_BF16), w_out[0][DIFF_WIDTH:].astype(_BF16), g_ffn,
          w_up[0].astype(_BF16), w_conv[0], b_conv, w_down[0].astype(_BF16))
    g_fin = g_final[None]

    pos_p = jnp.arange(S, dtype=jnp.int32)
    pos_s = P + jnp.arange(T, dtype=jnp.int32)
    y_p, st_p = _layer(x_prompt, pos_p, None, lam_init, wl, g_fin)
    past = (cache_diff_k[0], cache_diff_v[0], cache_mla_ckv[0], cache_mla_krope[0], state_conv[0])
    y_s, st_s = _layer(x_sample, pos_s, past, lam_init, wl, g_fin)
    return (y_p, y_s) + st_p + st_s
```

```python
import functools
import math

import jax
import jax.numpy as jnp
import numpy as np
from jax import lax
from jax.experimental import pallas as pl
from jax.experimental.pallas import tpu as pltpu

D_MODEL = 1024
CHUNK = 64
ROPE_THETA = 10000.0
NORM_EPS = 1e-6
NEG_INF = -1e30

DIFF_HEADS = 4
DIFF_HEAD_DIM = 64
DIFF_WIDTH = DIFF_HEADS * 2 * DIFF_HEAD_DIM
MLA_HEADS = 4
MLA_Q_LORA = 256
MLA_KV_LORA = 128
MLA_NOPE = 128
MLA_ROPE = 64
MLA_V = 128
MLA_WIDTH = MLA_HEADS * MLA_V
MLA_SCALE = (MLA_NOPE + MLA_ROPE) ** -0.5
OFF_DQ = 0
OFF_DK = OFF_DQ + DIFF_WIDTH
OFF_DV = OFF_DK + DIFF_WIDTH
OFF_CQ = OFF_DV + DIFF_WIDTH
OFF_CKV = OFF_CQ + MLA_Q_LORA
OFF_KR = OFF_CKV + MLA_KV_LORA
IN_COLS = OFF_KR + MLA_ROPE
IN_COLS_PAD = IN_COLS + MLA_ROPE
D_FF = 2816
CONV_W = 3

LANES = 128
MASK_COLS = 64
QK_W = 2 * LANES
MLA_QK = 2 * LANES
V_ONES = 2 * LANES
FF_CHUNK = 128
TRIP_SIZES = (2, 4, 6)
LOG2E = math.log2(math.e)
VMEM_LIMIT = 56 * 1024 * 1024

_BF16 = jnp.bfloat16
_F32 = jnp.float32


def _dot(a, b):
    return jnp.dot(a, b, preferred_element_type=_F32)


def _dot_t(a, b):
    return lax.dot_general(a, b, (((1,), (1,)), ((), ())), preferred_element_type=_F32)


def _rms(xf, g):
    return xf * lax.rsqrt(jnp.mean(xf * xf, axis=-1, keepdims=True) + NORM_EPS) * g


def _rope_slab(xs, cos2, sin_lo, sin_hi):
    return (xs * cos2 + pltpu.roll(xs, LANES - 32, axis=1) * sin_lo
            + pltpu.roll(xs, 32, axis=1) * sin_hi)


def _chunk_mask_cols(pos):
    lane = lax.broadcasted_iota(jnp.int32, (1, LANES), 1)
    chunk = lax.shift_right_logical(pos, CHUNK.bit_length() - 1)
    q_cols = jnp.where(lane == chunk, 1.0, 0.0)
    k_cols = jnp.where((chunk > lane) & (lane < MASK_COLS), NEG_INF, 0.0)
    return q_cols, k_cols


def _proj_kernel(x_ref, cos_ref, slo_ref, shi_ref, g_attn_ref, w_in_ref, g_qa_ref, w_qn_ref,
                 w_qr_ref, wkt_ref, g_kva_ref,
                 qd_ref, kd32_ref, kd16_ref, vd32_ref, vd16_ref, ckv32_ref, kr32_ref,
                 kmla_ref, vmla_ref, qmla_ref, *, tm, q_pos0):
    x = x_ref[0]
    h = _rms(x, g_attn_ref[...]).astype(_BF16)
    cos2, slo, shi = cos_ref[...], slo_ref[...], shi_ref[...]
    rope = functools.partial(_rope_slab, cos2=cos2, sin_lo=slo, sin_hi=shi)
    ones = jnp.ones((tm, LANES), _BF16)
    pos = q_pos0 + pl.program_id(1) * tm + lax.broadcasted_iota(jnp.int32, (tm, 1), 0)
    q_cols, k_cols = _chunk_mask_cols(pos)
    lane = lax.broadcasted_iota(jnp.int32, (1, LANES), 1)

    lat = _dot(h, w_in_ref[:, OFF_CQ:IN_COLS_PAD])
    c_q = _rms(lat[:, :MLA_Q_LORA], g_qa_ref[...]).astype(_BF16)
    c_kv = _rms(lat[:, MLA_Q_LORA:OFF_KR - OFF_CQ], g_kva_ref[...])
    kr_slab = rope(lat[:, OFF_KR - OFF_CQ:])

    proj = _dot(h, w_in_ref[:, :OFF_CQ])
    q_scale = DIFF_HEAD_DIM ** -0.5 * LOG2E
    for s in range(DIFF_HEADS):
        cs = slice(s * LANES, (s + 1) * LANES)
        qd_ref[0, :, s * QK_W:s * QK_W + LANES] = (
            rope(proj[:, OFF_DQ + s * LANES:OFF_DQ + (s + 1) * LANES]) * q_scale).astype(_BF16)
        qd_ref[0, :, s * QK_W + LANES:(s + 1) * QK_W] = q_cols.astype(_BF16)
        k = rope(proj[:, OFF_DK + s * LANES:OFF_DK + (s + 1) * LANES])
        kd32_ref[0, :, cs] = k
        kd16_ref[0, :, s * QK_W:s * QK_W + LANES] = k.astype(_BF16)
        kd16_ref[0, :, s * QK_W + LANES:(s + 1) * QK_W] = k_cols.astype(_BF16)
        v = proj[:, OFF_DV + s * LANES:OFF_DV + (s + 1) * LANES]
        vd32_ref[0, :, s, :] = v
        vd16_ref[0, :, s * V_ONES:s * V_ONES + LANES] = v.astype(_BF16)
        vd16_ref[0, :, s * V_ONES + LANES:(s + 1) * V_ONES] = ones

    ckv32_ref[0] = c_kv
    kr32_ref[0] = kr_slab[:, :MLA_ROPE]
    kmla_ref[0, :, :LANES] = c_kv.astype(_BF16)
    kmla_ref[0, :, LANES:] = jnp.where(lane < MASK_COLS, k_cols, kr_slab).astype(_BF16)
    vmla_ref[0, :, :LANES] = c_kv.astype(_BF16)
    vmla_ref[0, :, LANES:] = ones

    m_scale = MLA_SCALE * LOG2E
    q_nope = _dot(c_q, w_qn_ref[...]).astype(_BF16)
    q_rope = _dot(c_q, w_qr_ref[...])
    for hd in range(MLA_HEADS):
        q_lat = _dot(q_nope[:, hd * MLA_NOPE:(hd + 1) * MLA_NOPE], wkt_ref[hd])
        qmla_ref[0, hd, :, :LANES] = (q_lat * m_scale).astype(_BF16)
        if hd % 2 == 0:
            pair = rope(q_rope[:, (hd // 2) * LANES:(hd // 2 + 1) * LANES]) * m_scale
        own = pltpu.roll(pair, MLA_ROPE, axis=1) if hd % 2 == 0 else pair
        qmla_ref[0, hd, :, LANES:] = jnp.where(lane < MASK_COLS, q_cols, own).astype(_BF16)


def _proj_call(x, tabs, g_attn, w_in_p, g_qa, w_qn, w_qr, wkt, g_kva, *, tm, q_pos0):
    B, T, _ = x.shape
    nt = T // tm
    assert -(-(q_pos0 + T) // CHUNK) <= MASK_COLS, "chunk one-hot must fit the mask columns"
    row = lambda *c: pl.BlockSpec((1, tm) + c, lambda b, t: (b, t) + (0,) * len(c))
    tab = pl.BlockSpec((tm, LANES), lambda b, t: (t, 0))
    full = lambda a: pl.BlockSpec(a.shape, lambda b, t: (0,) * a.ndim)
    outs = (
        ((DIFF_HEADS * QK_W,), _BF16),
        ((DIFF_WIDTH,), _F32),
        ((DIFF_HEADS * QK_W,), _BF16),
        ((DIFF_HEADS, LANES), _F32),
        ((DIFF_HEADS * V_ONES,), _BF16),
        ((MLA_KV_LORA,), _F32),
        ((MLA_ROPE,), _F32),
        ((MLA_QK,), _BF16),
        ((V_ONES,), _BF16),
    )
    out_specs = tuple(row(*c) for c, _ in outs) + (
        pl.BlockSpec((1, MLA_HEADS, tm, MLA_QK), lambda b, t: (b, 0, t, 0)),)
    out_shape = tuple(jax.ShapeDtypeStruct((B, T) + c, d) for c, d in outs) + (
        jax.ShapeDtypeStruct((B, MLA_HEADS, T, MLA_QK), _BF16),)
    return pl.pallas_call(
        functools.partial(_proj_kernel, tm=tm, q_pos0=q_pos0),
        grid=(B, nt),
        in_specs=[row(D_MODEL), tab, tab, tab, full(g_attn), full(w_in_p), full(g_qa),
                  full(w_qn), full(w_qr), full(wkt), full(g_kva)],
        out_specs=out_specs,
        out_shape=out_shape,
        compiler_params=pltpu.CompilerParams(
            dimension_semantics=("arbitrary", "arbitrary"), vmem_limit_bytes=VMEM_LIMIT),
        name="proj",
    )(x, *tabs, g_attn, w_in_p, g_qa, w_qn, w_qr, wkt, g_kva)


def _attn_schedule(T, tq, tk, q_pos0, n_keys):
    pairs = []
    for qi in range(T // tq):
        last_pos = q_pos0 + (qi + 1) * tq - 1
        n_kv = -(-min((last_pos // CHUNK + 1) * CHUNK, n_keys) // tk)
        pairs += [(qi, kj, n_kv) for kj in range(n_kv)]
    per_trip = min(TRIP_SIZES, key=lambda n: (-len(pairs) % n, -n))
    items = [(qi, kj, int(kj == 0), int(kj == n_kv - 1), qi % per_trip) for qi, kj, n_kv in pairs]
    trips = -(-len(items) // per_trip)
    items += [(0, 0, 1, 0, per_trip)] * (per_trip * trips + 1 - len(items))
    return np.asarray(items, np.int32).T.copy(), trips, per_trip


def _flash(tab_ref, load_q, k_ref, v_ref, emit, scratch, *, trips, first_step):
    (s0, s1, m0, m1, a0, a1, acc_ref) = scratch
    tk = s0.shape[1]

    def score(w, s_ref, m_ref, a_ref, m_prev_ref):
        k0 = pl.multiple_of(tab_ref[1, w] * tk, tk)
        s = _dot_t(load_q(tab_ref[0, w]), k_ref[0, pl.ds(k0, tk), :])
        m_prev = jnp.where(tab_ref[2, w] > 0, NEG_INF, m_prev_ref[...])
        m_new = jnp.maximum(m_prev, jnp.max(s, axis=1, keepdims=True))
        s_ref[...] = s
        a_ref[...] = jnp.exp2(m_prev - m_new)
        m_ref[...] = m_new

    def finish(w, s_ref, m_ref, a_ref):
        k0 = pl.multiple_of(tab_ref[1, w] * tk, tk)
        slot = tab_ref[4, w]
        p = jnp.exp2(s_ref[...] - jnp.tile(m_ref[...], (1, tk // LANES)))
        pv = _dot(p.astype(_BF16), v_ref[0, pl.ds(k0, tk), :])
        acc_ref[slot] = jnp.tile(a_ref[...], (1, V_ONES // LANES)) * acc_ref[slot] + pv

    def emit_if_last(w):
        @pl.when(tab_ref[3, w] > 0)
        def _():
            acc = acc_ref[tab_ref[4, w]]
            emit(tab_ref[0, w], acc[:, :LANES] / acc[:, LANES:])

    @pl.when(first_step)
    def _():
        acc_ref[...] = jnp.zeros(acc_ref.shape, _F32)

    m1[...] = jnp.full(m1.shape, NEG_INF, _F32)
    score(0, s0, m0, a0, m1)

    even, odd = (s0, m0, a0), (s1, m1, a1)
    per_trip = acc_ref.shape[0] - 1

    def trip(t, carry):
        w = per_trip * t
        for i in range(0, per_trip, 2):
            score(w + i + 1, *odd, m0)
            finish(w + i, *even)
            score(w + i + 2, *even, m1)
            finish(w + i + 1, *odd)
        for i in range(per_trip):
            emit_if_last(w + i)
        return carry

    lax.fori_loop(0, trips, trip, 0)


def _attn_scratch(rows, tk, per_trip):
    s = pltpu.VMEM((rows, tk), _F32)
    r = pltpu.VMEM((rows, LANES), _F32)
    return [s, s, r, r, r, r, pltpu.VMEM((per_trip + 1, rows, V_ONES), _F32)]


def _diff_attn_kernel(tab_ref, q_ref, k_ref, v_ref, lam_ref, o_ref, *scratch, tq, trips,
                      lam_init):
    lane = lax.broadcasted_iota(jnp.int32, (1, QK_W), 1)
    lp = lam_ref[...]
    lam = (jnp.exp(jnp.sum(lp[0:1] * lp[1:2], axis=1, keepdims=True))
           - jnp.exp(jnp.sum(lp[2:3] * lp[3:4], axis=1, keepdims=True)) + lam_init)

    def load_q(qi):
        q = q_ref[0, pl.ds(pl.multiple_of(qi * tq, tq), tq), :]
        zero = jnp.zeros_like(q)
        return jnp.concatenate(
            [jnp.where((lane < DIFF_HEAD_DIM) | (lane >= LANES), q, zero),
             jnp.where(lane >= DIFF_HEAD_DIM, q, zero)], axis=0)

    def emit(qi, o):
        o_ref[0, pl.ds(pl.multiple_of(qi * tq, tq), tq), :] = (
            o[:tq] - lam * o[tq:]).astype(o_ref.dtype)

    first_step = (pl.program_id(0) == 0) & (pl.program_id(1) == 0)
    _flash(tab_ref, load_q, k_ref, v_ref, emit, scratch, trips=trips, first_step=first_step)


def _diff_attn_call(qd, kd, vd, lam_p, *, tq, tk, q_pos0, n_keys, lam_init):
    B, T, _ = qd.shape
    Sk = kd.shape[1]
    tab, trips, per_trip = _attn_schedule(T, tq, tk, q_pos0, n_keys)
    kern = functools.partial(_diff_attn_kernel, tq=tq, trips=trips, lam_init=lam_init)
    head = lambda rows, c: pl.BlockSpec((1, rows, c), lambda b, h, tab: (b, 0, h))
    return pl.pallas_call(
        kern,
        grid_spec=pltpu.PrefetchScalarGridSpec(
            num_scalar_prefetch=1,
            grid=(B, DIFF_HEADS),
            in_specs=[head(T, QK_W), head(Sk, QK_W), head(Sk, V_ONES),
                      pl.BlockSpec(lam_p.shape, lambda b, h, tab: (0, 0))],
            out_specs=head(T, LANES),
            scratch_shapes=_attn_scratch(2 * tq, tk, per_trip)),
        out_shape=jax.ShapeDtypeStruct((B, T, DIFF_WIDTH), _BF16),
        compiler_params=pltpu.CompilerParams(
            dimension_semantics=("arbitrary",) * 2, vmem_limit_bytes=VMEM_LIMIT),
        name="diff_attn",
    )(jnp.asarray(tab), qd, kd, vd, lam_p)


def _mla_attn_kernel(tab_ref, q_ref, k_ref, v_ref, o_ref, *scratch, tq, trips):
    def load_q(qi):
        q = q_ref[0, :, pl.ds(pl.multiple_of(qi * tq, tq), tq), :]
        return q.reshape(MLA_HEADS * tq, MLA_QK)

    def emit(qi, o):
        o = o.astype(o_ref.dtype)
        for h in range(MLA_HEADS):
            o_ref[0, pl.ds(pl.multiple_of(qi * tq, tq), tq), h * LANES:(h + 1) * LANES] = (
                o[h * tq:(h + 1) * tq])

    _flash(tab_ref, load_q, k_ref, v_ref, emit, scratch, trips=trips,
           first_step=pl.program_id(0) == 0)


def _mla_attn_call(qm, kk, vv, *, tq, tk, q_pos0, n_keys):
    B, _, T, _ = qm.shape
    Sk = kk.shape[1]
    tab, trips, per_trip = _attn_schedule(T, tq, tk, q_pos0, n_keys)
    kern = functools.partial(_mla_attn_kernel, tq=tq, trips=trips)
    kv = pl.BlockSpec((1, Sk, MLA_QK), lambda b, tab: (b, 0, 0))
    return pl.pallas_call(
        kern,
        grid_spec=pltpu.PrefetchScalarGridSpec(
            num_scalar_prefetch=1,
            grid=(B,),
            in_specs=[pl.BlockSpec((1, MLA_HEADS, T, MLA_QK), lambda b, tab: (b, 0, 0, 0)),
                      kv, kv],
            out_specs=pl.BlockSpec((1, T, MLA_HEADS * LANES), lambda b, tab: (b, 0, 0)),
            scratch_shapes=_attn_scratch(MLA_HEADS * tq, tk, per_trip)),
        out_shape=jax.ShapeDtypeStruct((B, T, MLA_HEADS * LANES), _BF16),
        compiler_params=pltpu.CompilerParams(
            dimension_semantics=("arbitrary",), vmem_limit_bytes=VMEM_LIMIT),
        name="mla_attn",
    )(jnp.asarray(tab), qm, kk, vv)


def _ffn_kernel(x_ref, od_ref, om_ref, cprev_ref, gsub_ref, wv_ref, w_od_ref, w_om_ref,
                g_ffn_ref, w_up_ref, w_conv_ref, b_conv_ref, w_down_ref, g_fin_ref,
                y_ref, nconv_ref, carry_ref, gbuf_ref, act_ref, *, tm, lam_init):
    t = pl.program_id(1)

    @pl.when(t == 0)
    def _():
        carry_ref[...] = jnp.zeros(carry_ref.shape, _F32)
        carry_ref[8 - (CONV_W - 1):, :] = cprev_ref[0]

    mix_d = jnp.concatenate(
        [(_rms(od_ref[0, :, h * LANES:(h + 1) * LANES].astype(_F32), gsub_ref[...])
          * (1.0 - lam_init)).astype(_BF16) for h in range(DIFF_HEADS)], axis=1)
    mix_m = jnp.concatenate(
        [_dot(om_ref[0, :, h * LANES:(h + 1) * LANES], wv_ref[h]).astype(_BF16)
         for h in range(MLA_HEADS)], axis=1)
    x1 = x_ref[0] + _dot(mix_d, w_od_ref[...]) + _dot(mix_m, w_om_ref[...])
    h2 = _rms(x1, g_ffn_ref[...]).astype(_BF16)
    for c in range(D_FF // FF_CHUNK):
        cs = slice(c * FF_CHUNK, (c + 1) * FF_CHUNK)
        ug = _dot(h2, w_up_ref[:, 2 * c * FF_CHUNK:2 * (c + 1) * FF_CHUNK])
        u, g = ug[:, :FF_CHUNK], ug[:, FF_CHUNK:]
        gbuf_ref[0:8, :] = carry_ref[:, cs]
        gbuf_ref[8:, :] = g
        conv = (b_conv_ref[:, cs] + gbuf_ref[6:6 + tm, :] * w_conv_ref[0:1, cs]
                + gbuf_ref[7:7 + tm, :] * w_conv_ref[1:2, cs] + g * w_conv_ref[2:3, cs])
        half = 0.5 * conv
        act_ref[:, cs] = ((half + half * jnp.tanh(half)) * u).astype(_BF16)
        carry_ref[:, cs] = g[tm - 8:, :]
    nconv_ref[0] = carry_ref[8 - (CONV_W - 1):, :]
    y_ref[0] = _rms(x1 + _dot(act_ref[...], w_down_ref[...]), g_fin_ref[...])


def _ffn_call(x, o_d, o_m, conv_prev, g_sub, wv, w_od, w_om, g_ffn, w_up, w_conv, b_conv,
              w_down, g_fin, *, tm, lam_init):
    B, T, _ = x.shape
    row = lambda c: pl.BlockSpec((1, tm, c), lambda b, t: (b, t, 0))
    full = lambda a: pl.BlockSpec(a.shape, lambda b, t: (0,) * a.ndim,
                                  pipeline_mode=pl.Buffered(1))
    state = pl.BlockSpec((1, CONV_W - 1, D_FF), lambda b, t: (b, 0, 0))
    weights = (g_sub, wv, w_od, w_om, g_ffn, w_up, w_conv, b_conv, w_down, g_fin)
    return pl.pallas_call(
        functools.partial(_ffn_kernel, tm=tm, lam_init=lam_init),
        grid=(B, T // tm),
        in_specs=[row(D_MODEL), row(DIFF_WIDTH), row(MLA_WIDTH), state]
                 + [full(w) for w in weights],
        out_specs=(row(D_MODEL), state),
        out_shape=(jax.ShapeDtypeStruct((B, T, D_MODEL), _F32),
                   jax.ShapeDtypeStruct((B, CONV_W - 1, D_FF), _F32)),
        scratch_shapes=[pltpu.VMEM((8, D_FF), _F32), pltpu.VMEM((tm + 8, FF_CHUNK), _F32),
                        pltpu.VMEM((tm, D_FF), _BF16)],
        compiler_params=pltpu.CompilerParams(
            dimension_semantics=("arbitrary", "arbitrary"), vmem_limit_bytes=VMEM_LIMIT),
        name="ffn",
    )(x, o_d, o_m, conv_prev, *weights)


def _rope_tables(pos):
    half = DIFF_HEAD_DIM // 2
    inv = ROPE_THETA ** (-jnp.arange(half, dtype=_F32) * (2.0 / DIFF_HEAD_DIM))
    ang = pos.astype(_F32)[:, None] * inv[None, :]
    cos, sin, zero = jnp.cos(ang), jnp.sin(ang), jnp.zeros_like(ang)
    cos2 = jnp.concatenate([cos, cos, cos, cos], axis=1)
    sin_lo = jnp.concatenate([-sin, zero, -sin, zero], axis=1)
    sin_hi = jnp.concatenate([zero, sin, zero, sin], axis=1)
    return cos2, sin_lo, sin_hi


def _tiles(T, n_keys):
    if T >= 512:
        return dict(tm=512, diff=(512, 512), mla=(256, 512))
    tk = -(-n_keys // LANES) * LANES
    return dict(tm=T, diff=(T, tk), mla=(T, tk))


def _layer(x, pos, past, lam_init, wl, g_final):
    (g_attn, w_in_p, lam_p, g_sub, g_qa, w_qn, w_qr, wkt, wv, g_kva, w_od, w_om, g_ffn, w_up,
     w_conv, b_conv, w_down) = wl
    B, T, _ = x.shape
    n_past = 0 if past is None else past[0].shape[1]
    n_keys, q_pos0 = n_past + T, n_past
    tl = _tiles(T, n_keys)
    tabs = _rope_tables(pos)
    qd, kd32, kd16, vd32, vd16, ckv32, kr32, kmla, vmla, qmla = _proj_call(
        x, tabs, g_attn, w_in_p, g_qa, w_qn, w_qr, wkt, g_kva, tm=tl["tm"], q_pos0=q_pos0)

    if past is None:
        kd_all, vd_all, kmla_all, vmla_all = kd16, vd16, kmla, vmla
        conv_prev = jnp.zeros((B, CONV_W - 1, D_FF), _F32)
    else:
        p_dk, p_dv, p_ckv, p_kr, conv_prev = past
        n_pad = tl["diff"][1] - n_keys
        lane = jnp.arange(LANES)[None, :]
        k_cols = jnp.where(((jnp.arange(n_past) // CHUNK)[:, None] > lane) & (lane < MASK_COLS),
                           NEG_INF, 0.0).astype(_BF16)
        k_cols = jnp.broadcast_to(k_cols, (B, n_past, LANES))
        pad_cols = jnp.where(lane < MASK_COLS, NEG_INF, 0.0).astype(_BF16)
        one = jnp.ones((B, n_past, LANES), _BF16)
        p_dk = p_dk.reshape(B, n_past, DIFF_WIDTH).astype(_BF16)
        p_dv = p_dv.reshape(B, n_past, DIFF_WIDTH).astype(_BF16)
        heads = lambda a: [a[:, :, h * LANES:(h + 1) * LANES] for h in range(DIFF_HEADS)]

        def cat(old, new, pad_row):
            pad = jnp.broadcast_to(pad_row, (B, n_pad, new.shape[-1]))
            return jnp.concatenate([jnp.concatenate(old, axis=-1), new, pad], axis=1)

        zero = lambda w: jnp.zeros((1, 1, w), _BF16)
        kd_all = cat([p for a in heads(p_dk) for p in (a, k_cols)], kd16,
                     jnp.tile(jnp.concatenate([zero(LANES)[0], pad_cols], axis=-1), DIFF_HEADS))
        vd_all = cat([p for a in heads(p_dv) for p in (a, one)], vd16, zero(DIFF_HEADS * V_ONES))
        kmla_all = cat([p_ckv.astype(_BF16), k_cols[:, :, :MASK_COLS], p_kr.astype(_BF16)], kmla,
                       jnp.concatenate([zero(LANES)[0], pad_cols[:, :MASK_COLS],
                                        zero(MLA_ROPE)[0]], -1))
        vmla_all = cat([p_ckv.astype(_BF16), one], vmla, zero(V_ONES))

    o_d = _diff_attn_call(qd, kd_all, vd_all, lam_p, tq=tl["diff"][0], tk=tl["diff"][1],
                          q_pos0=q_pos0, n_keys=n_keys, lam_init=lam_init)
    o_m = _mla_attn_call(qmla, kmla_all, vmla_all, tq=tl["mla"][0], tk=tl["mla"][1],
                         q_pos0=q_pos0, n_keys=n_keys)
    y, new_conv = _ffn_call(x, o_d, o_m, conv_prev, g_sub, wv, w_od, w_om, g_ffn, w_up, w_conv,
                            b_conv, w_down, g_final, tm=tl["tm"], lam_init=lam_init)
    state = (kd32.reshape(1, B, T, DIFF_HEADS, 2, DIFF_HEAD_DIM),
             vd32[None],
             ckv32[None], kr32[None], new_conv[None])
    return y, state


def kernel(x_prompt, x_sample, cache_diff_k, cache_diff_v, cache_mla_ckv, cache_mla_krope,
           state_conv, g_attn, w_in, lambda_q1, lambda_k1, lambda_q2, lambda_k2, g_diff_sub,
           g_q_lora, w_q_b, g_kv_lora, w_kv_b, w_out, g_ffn, w_up, w_conv, b_conv, w_down,
           g_final):
    assert g_attn.shape[0] == 1, "single-layer model"
    S = x_prompt.shape[1]
    T = x_sample.shape[1]
    P = cache_diff_k.shape[2]
    lam_init = 0.8 - 0.6 * math.exp(-0.3 * 0)

    w_in_p = jnp.concatenate([w_in[0], w_in[0][:, OFF_KR:IN_COLS]], axis=1).astype(_BF16)
    wq = w_q_b[0].reshape(MLA_Q_LORA, MLA_HEADS, MLA_NOPE + MLA_ROPE)
    w_qn = wq[:, :, :MLA_NOPE].reshape(MLA_Q_LORA, MLA_HEADS * MLA_NOPE).astype(_BF16)
    w_qr = wq[:, :, MLA_NOPE:].reshape(MLA_Q_LORA, MLA_HEADS * MLA_ROPE).astype(_BF16)
    wkv = w_kv_b[0].reshape(MLA_KV_LORA, MLA_HEADS, MLA_NOPE + MLA_V)
    wkt = jnp.transpose(wkv[:, :, :MLA_NOPE], (1, 2, 0)).astype(_BF16)
    wv = jnp.transpose(wkv[:, :, MLA_NOPE:], (1, 0, 2)).astype(_BF16)
    lam_p = jnp.concatenate([lambda_q1, lambda_k1, lambda_q2, lambda_k2], axis=0)
    w_ug = jnp.transpose(w_up[0].reshape(D_MODEL, 2, D_FF // FF_CHUNK, FF_CHUNK), (0, 2, 1, 3))
    w_ug = w_ug.reshape(D_MODEL, 2 * D_FF).astype(_BF16)
    wl = (g_attn, w_in_p, lam_p, g_diff_sub, g_q_lora, w_qn, w_qr, wkt, wv, g_kv_lora,
          w_out[0][:DIFF_WIDTH].astype(_BF16), w_out[0][DIFF_WIDTH:].astype(_BF16), g_ffn,
          w_ug, w_conv[0], b_conv, w_down[0].astype(_BF16))
    g_fin = g_final[None]

    pos_p = jnp.arange(S, dtype=jnp.int32)
    pos_s = P + jnp.arange(T, dtype=jnp.int32)
    y_p, st_p = _layer(x_prompt, pos_p, None, lam_init, wl, g_fin)
    past = (cache_diff_k[0], cache_diff_v[0], cache_mla_ckv[0], cache_mla_krope[0], state_conv[0])
    y_s, st_s = _layer(x_sample, pos_s, past, lam_init, wl, g_fin)
    return (y_p, y_s) + st_p + st_s
```

```python
import functools
import math

import jax
import jax.numpy as jnp
import numpy as np
from jax import lax
from jax.experimental import pallas as pl
from jax.experimental.pallas import tpu as pltpu

D_MODEL = 1024
CHUNK = 64
ROPE_THETA = 10000.0
NORM_EPS = 1e-6
NEG_INF = -1e30

DIFF_HEADS = 4
DIFF_HEAD_DIM = 64
DIFF_WIDTH = DIFF_HEADS * 2 * DIFF_HEAD_DIM
MLA_HEADS = 4
MLA_Q_LORA = 256
MLA_KV_LORA = 128
MLA_NOPE = 128
MLA_ROPE = 64
MLA_V = 128
MLA_WIDTH = MLA_HEADS * MLA_V
MLA_SCALE = (MLA_NOPE + MLA_ROPE) ** -0.5
OFF_DQ = 0
OFF_DK = OFF_DQ + DIFF_WIDTH
OFF_DV = OFF_DK + DIFF_WIDTH
OFF_CQ = OFF_DV + DIFF_WIDTH
OFF_CKV = OFF_CQ + MLA_Q_LORA
OFF_KR = OFF_CKV + MLA_KV_LORA
IN_COLS = OFF_KR + MLA_ROPE
IN_COLS_PAD = IN_COLS + MLA_ROPE
D_FF = 2816
CONV_W = 3

LANES = 128
MASK_COLS = 64
QK_W = 2 * LANES
MLA_QK = 2 * LANES
V_ONES = 2 * LANES
FF_CHUNK = 256
TRIP_SIZES = (2, 4, 6, 12)
LOG2E = math.log2(math.e)
VMEM_LIMIT = 56 * 1024 * 1024

_BF16 = jnp.bfloat16
_F32 = jnp.float32


def _dot(a, b):
    return jnp.dot(a, b, preferred_element_type=_F32)


def _dot_t(a, b):
    return lax.dot_general(a, b, (((1,), (1,)), ((), ())), preferred_element_type=_F32)


def _rms(xf, g):
    return xf * lax.rsqrt(jnp.mean(xf * xf, axis=-1, keepdims=True) + NORM_EPS) * g


def _rope_slab(xs, cos2, sin_lo, sin_hi):
    return (xs * cos2 + pltpu.roll(xs, LANES - 32, axis=1) * sin_lo
            + pltpu.roll(xs, 32, axis=1) * sin_hi)


def _chunk_mask_cols(pos):
    lane = lax.broadcasted_iota(jnp.int32, (1, LANES), 1)
    chunk = lax.shift_right_logical(pos, CHUNK.bit_length() - 1)
    q_cols = jnp.where(lane == chunk, 1.0, 0.0)
    k_cols = jnp.where((chunk > lane) & (lane < MASK_COLS), NEG_INF, 0.0)
    return q_cols, k_cols


def _proj_kernel(x_ref, cos_ref, slo_ref, shi_ref, g_attn_ref, w_in_ref, g_qa_ref, w_qn_ref,
                 w_qr_ref, wkt_ref, g_kva_ref,
                 qd_ref, kd32_ref, kd16_ref, vd32_ref, vd16_ref, ckv32_ref, kr32_ref,
                 kmla_ref, vmla_ref, qmla_ref, *, tm, q_pos0):
    x = x_ref[0]
    h = _rms(x, g_attn_ref[...]).astype(_BF16)
    cos2, slo, shi = cos_ref[...], slo_ref[...], shi_ref[...]
    rope = functools.partial(_rope_slab, cos2=cos2, sin_lo=slo, sin_hi=shi)
    ones = jnp.ones((tm, LANES), _BF16)
    pos = q_pos0 + pl.program_id(1) * tm + lax.broadcasted_iota(jnp.int32, (tm, 1), 0)
    q_cols, k_cols = _chunk_mask_cols(pos)
    lane = lax.broadcasted_iota(jnp.int32, (1, LANES), 1)

    lat = _dot(h, w_in_ref[:, OFF_CQ:IN_COLS_PAD])
    c_q = _rms(lat[:, :MLA_Q_LORA], g_qa_ref[...]).astype(_BF16)
    c_kv = _rms(lat[:, MLA_Q_LORA:OFF_KR - OFF_CQ], g_kva_ref[...])
    kr_slab = rope(lat[:, OFF_KR - OFF_CQ:])

    proj = _dot(h, w_in_ref[:, :OFF_CQ])
    q_scale = DIFF_HEAD_DIM ** -0.5 * LOG2E
    for s in range(DIFF_HEADS):
        cs = slice(s * LANES, (s + 1) * LANES)
        qd_ref[0, :, s * QK_W:s * QK_W + LANES] = (
            rope(proj[:, OFF_DQ + s * LANES:OFF_DQ + (s + 1) * LANES]) * q_scale).astype(_BF16)
        qd_ref[0, :, s * QK_W + LANES:(s + 1) * QK_W] = q_cols.astype(_BF16)
        k = rope(proj[:, OFF_DK + s * LANES:OFF_DK + (s + 1) * LANES])
        kd32_ref[0, :, cs] = k
        kd16_ref[0, :, s * QK_W:s * QK_W + LANES] = k.astype(_BF16)
        kd16_ref[0, :, s * QK_W + LANES:(s + 1) * QK_W] = k_cols.astype(_BF16)
        v = proj[:, OFF_DV + s * LANES:OFF_DV + (s + 1) * LANES]
        vd32_ref[0, :, s, :] = v
        vd16_ref[0, :, s * V_ONES:s * V_ONES + LANES] = v.astype(_BF16)
        vd16_ref[0, :, s * V_ONES + LANES:(s + 1) * V_ONES] = ones

    ckv32_ref[0] = c_kv
    kr32_ref[0] = kr_slab[:, :MLA_ROPE]
    kmla_ref[0, :, :LANES] = c_kv.astype(_BF16)
    kmla_ref[0, :, LANES:] = jnp.where(lane < MASK_COLS, k_cols, kr_slab).astype(_BF16)
    vmla_ref[0, :, :LANES] = c_kv.astype(_BF16)
    vmla_ref[0, :, LANES:] = ones

    m_scale = MLA_SCALE * LOG2E
    q_nope = _dot(c_q, w_qn_ref[...]).astype(_BF16)
    q_rope = _dot(c_q, w_qr_ref[...])
    for hd in range(MLA_HEADS):
        q_lat = _dot(q_nope[:, hd * MLA_NOPE:(hd + 1) * MLA_NOPE], wkt_ref[hd])
        qmla_ref[0, hd, :, :LANES] = (q_lat * m_scale).astype(_BF16)
        if hd % 2 == 0:
            pair = rope(q_rope[:, (hd // 2) * LANES:(hd // 2 + 1) * LANES]) * m_scale
        own = pltpu.roll(pair, MLA_ROPE, axis=1) if hd % 2 == 0 else pair
        qmla_ref[0, hd, :, LANES:] = jnp.where(lane < MASK_COLS, q_cols, own).astype(_BF16)


def _proj_call(x, tabs, g_attn, w_in_p, g_qa, w_qn, w_qr, wkt, g_kva, *, tm, q_pos0):
    B, T, _ = x.shape
    nt = T // tm
    assert -(-(q_pos0 + T) // CHUNK) <= MASK_COLS, "chunk one-hot must fit the mask columns"
    row = lambda *c: pl.BlockSpec((1, tm) + c, lambda b, t: (b, t) + (0,) * len(c))
    tab = pl.BlockSpec((tm, LANES), lambda b, t: (t, 0))
    full = lambda a: pl.BlockSpec(a.shape, lambda b, t: (0,) * a.ndim)
    outs = (
        ((DIFF_HEADS * QK_W,), _BF16),
        ((DIFF_WIDTH,), _F32),
        ((DIFF_HEADS * QK_W,), _BF16),
        ((DIFF_HEADS, LANES), _F32),
        ((DIFF_HEADS * V_ONES,), _BF16),
        ((MLA_KV_LORA,), _F32),
        ((MLA_ROPE,), _F32),
        ((MLA_QK,), _BF16),
        ((V_ONES,), _BF16),
    )
    out_specs = tuple(row(*c) for c, _ in outs) + (
        pl.BlockSpec((1, MLA_HEADS, tm, MLA_QK), lambda b, t: (b, 0, t, 0)),)
    out_shape = tuple(jax.ShapeDtypeStruct((B, T) + c, d) for c, d in outs) + (
        jax.ShapeDtypeStruct((B, MLA_HEADS, T, MLA_QK), _BF16),)
    return pl.pallas_call(
        functools.partial(_proj_kernel, tm=tm, q_pos0=q_pos0),
        grid=(B, nt),
        in_specs=[row(D_MODEL), tab, tab, tab, full(g_attn), full(w_in_p), full(g_qa),
                  full(w_qn), full(w_qr), full(wkt), full(g_kva)],
        out_specs=out_specs,
        out_shape=out_shape,
        compiler_params=pltpu.CompilerParams(
            dimension_semantics=("arbitrary", "arbitrary"), vmem_limit_bytes=VMEM_LIMIT),
        name="proj",
    )(x, *tabs, g_attn, w_in_p, g_qa, w_qn, w_qr, wkt, g_kva)


def _attn_schedule(T, tq, tk, q_pos0, n_keys):
    pairs = []
    for qi in range(T // tq):
        last_pos = q_pos0 + (qi + 1) * tq - 1
        n_kv = -(-min((last_pos // CHUNK + 1) * CHUNK, n_keys) // tk)
        pairs += [(qi, kj, n_kv) for kj in range(n_kv)]
    per_trip = min(TRIP_SIZES, key=lambda n: (-len(pairs) % n, -n))
    items = [(qi, kj, int(kj == 0), int(kj == n_kv - 1), qi % per_trip) for qi, kj, n_kv in pairs]
    trips = -(-len(items) // per_trip)
    items += [(0, 0, 1, 0, per_trip)] * (per_trip * trips + 1 - len(items))
    return np.asarray(items, np.int32).T.copy(), trips, per_trip


def _flash(tab_ref, load_q, k_ref, v_ref, emit, scratch, *, trips, first_step):
    (s0, s1, m0, m1, a0, a1, acc_ref) = scratch
    tk = s0.shape[1]

    def score(w, s_ref, m_ref, a_ref, m_prev_ref):
        k0 = pl.multiple_of(tab_ref[1, w] * tk, tk)
        s = _dot_t(load_q(tab_ref[0, w]), k_ref[0, pl.ds(k0, tk), :])
        m_prev = jnp.where(tab_ref[2, w] > 0, NEG_INF, m_prev_ref[...])
        m_new = jnp.maximum(m_prev, jnp.max(s, axis=1, keepdims=True))
        s_ref[...] = s
        a_ref[...] = jnp.exp2(m_prev - m_new)
        m_ref[...] = m_new

    def finish(w, s_ref, m_ref, a_ref):
        k0 = pl.multiple_of(tab_ref[1, w] * tk, tk)
        slot = tab_ref[4, w]
        p = jnp.exp2(s_ref[...] - jnp.tile(m_ref[...], (1, tk // LANES)))
        pv = _dot(p.astype(_BF16), v_ref[0, pl.ds(k0, tk), :])
        acc_ref[slot] = jnp.tile(a_ref[...], (1, V_ONES // LANES)) * acc_ref[slot] + pv

    def emit_if_last(w):
        @pl.when(tab_ref[3, w] > 0)
        def _():
            acc = acc_ref[tab_ref[4, w]]
            emit(tab_ref[0, w], acc[:, :LANES] / acc[:, LANES:])

    @pl.when(first_step)
    def _():
        acc_ref[...] = jnp.zeros(acc_ref.shape, _F32)

    m1[...] = jnp.full(m1.shape, NEG_INF, _F32)
    score(0, s0, m0, a0, m1)

    even, odd = (s0, m0, a0), (s1, m1, a1)
    per_trip = acc_ref.shape[0] - 1

    def trip(t, carry):
        w = per_trip * t
        for i in range(0, per_trip, 2):
            score(w + i + 1, *odd, m0)
            finish(w + i, *even)
            score(w + i + 2, *even, m1)
            finish(w + i + 1, *odd)
        for i in range(per_trip):
            emit_if_last(w + i)
        return carry

    lax.fori_loop(0, trips, trip, 0)


def _attn_scratch(rows, tk, per_trip):
    s = pltpu.VMEM((rows, tk), _F32)
    r = pltpu.VMEM((rows, LANES), _F32)
    return [s, s, r, r, r, r, pltpu.VMEM((per_trip + 1, rows, V_ONES), _F32)]


def _diff_attn_kernel(tab_ref, q_ref, k_ref, v_ref, lam_ref, o_ref, *scratch, tq, trips,
                      lam_init):
    lane = lax.broadcasted_iota(jnp.int32, (1, QK_W), 1)
    lp = lam_ref[...]
    lam = (jnp.exp(jnp.sum(lp[0:1] * lp[1:2], axis=1, keepdims=True))
           - jnp.exp(jnp.sum(lp[2:3] * lp[3:4], axis=1, keepdims=True)) + lam_init)

    def load_q(qi):
        q = q_ref[0, pl.ds(pl.multiple_of(qi * tq, tq), tq), :]
        zero = jnp.zeros_like(q)
        return jnp.concatenate(
            [jnp.where((lane < DIFF_HEAD_DIM) | (lane >= LANES), q, zero),
             jnp.where(lane >= DIFF_HEAD_DIM, q, zero)], axis=0)

    def emit(qi, o):
        o_ref[0, pl.ds(pl.multiple_of(qi * tq, tq), tq), :] = (
            o[:tq] - lam * o[tq:]).astype(o_ref.dtype)

    first_step = (pl.program_id(0) == 0) & (pl.program_id(1) == 0)
    _flash(tab_ref, load_q, k_ref, v_ref, emit, scratch, trips=trips, first_step=first_step)


def _diff_attn_call(qd, kd, vd, lam_p, *, tq, tk, q_pos0, n_keys, lam_init):
    B, T, _ = qd.shape
    Sk = kd.shape[1]
    tab, trips, per_trip = _attn_schedule(T, tq, tk, q_pos0, n_keys)
    kern = functools.partial(_diff_attn_kernel, tq=tq, trips=trips, lam_init=lam_init)
    head = lambda rows, c: pl.BlockSpec((1, rows, c), lambda b, h, tab: (b, 0, h))
    return pl.pallas_call(
        kern,
        grid_spec=pltpu.PrefetchScalarGridSpec(
            num_scalar_prefetch=1,
            grid=(B, DIFF_HEADS),
            in_specs=[head(T, QK_W), head(Sk, QK_W), head(Sk, V_ONES),
                      pl.BlockSpec(lam_p.shape, lambda b, h, tab: (0, 0))],
            out_specs=head(T, LANES),
            scratch_shapes=_attn_scratch(2 * tq, tk, per_trip)),
        out_shape=jax.ShapeDtypeStruct((B, T, DIFF_WIDTH), _BF16),
        compiler_params=pltpu.CompilerParams(
            dimension_semantics=("arbitrary",) * 2, vmem_limit_bytes=VMEM_LIMIT),
        name="diff_attn",
    )(jnp.asarray(tab), qd, kd, vd, lam_p)


def _mla_attn_kernel(tab_ref, q_ref, k_ref, v_ref, o_ref, *scratch, tq, trips):
    def load_q(qi):
        q = q_ref[0, :, pl.ds(pl.multiple_of(qi * tq, tq), tq), :]
        return q.reshape(MLA_HEADS * tq, MLA_QK)

    def emit(qi, o):
        o = o.astype(o_ref.dtype)
        for h in range(MLA_HEADS):
            o_ref[0, pl.ds(pl.multiple_of(qi * tq, tq), tq), h * LANES:(h + 1) * LANES] = (
                o[h * tq:(h + 1) * tq])

    _flash(tab_ref, load_q, k_ref, v_ref, emit, scratch, trips=trips,
           first_step=pl.program_id(0) == 0)


def _mla_attn_call(qm, kk, vv, *, tq, tk, q_pos0, n_keys):
    B, _, T, _ = qm.shape
    Sk = kk.shape[1]
    tab, trips, per_trip = _attn_schedule(T, tq, tk, q_pos0, n_keys)
    kern = functools.partial(_mla_attn_kernel, tq=tq, trips=trips)
    kv = pl.BlockSpec((1, Sk, MLA_QK), lambda b, tab: (b, 0, 0))
    return pl.pallas_call(
        kern,
        grid_spec=pltpu.PrefetchScalarGridSpec(
            num_scalar_prefetch=1,
            grid=(B,),
            in_specs=[pl.BlockSpec((1, MLA_HEADS, T, MLA_QK), lambda b, tab: (b, 0, 0, 0)),
                      kv, kv],
            out_specs=pl.BlockSpec((1, T, MLA_HEADS * LANES), lambda b, tab: (b, 0, 0)),
            scratch_shapes=_attn_scratch(MLA_HEADS * tq, tk, per_trip)),
        out_shape=jax.ShapeDtypeStruct((B, T, MLA_HEADS * LANES), _BF16),
        compiler_params=pltpu.CompilerParams(
            dimension_semantics=("arbitrary",), vmem_limit_bytes=VMEM_LIMIT),
        name="mla_attn",
    )(jnp.asarray(tab), qm, kk, vv)


def _ffn_kernel(x_ref, od_ref, om_ref, cprev_ref, gsub_ref, wv_ref, w_od_ref, w_om_ref,
                g_ffn_ref, w_up_ref, w_conv_ref, b_conv_ref, w_down_ref, g_fin_ref,
                y_ref, nconv_ref, carry_ref, gbuf_ref, act_ref, *, tm, lam_init):
    t = pl.program_id(1)

    @pl.when(t == 0)
    def _():
        carry_ref[...] = jnp.zeros(carry_ref.shape, _F32)
        carry_ref[8 - (CONV_W - 1):, :] = cprev_ref[0]

    mix_d = jnp.concatenate(
        [(_rms(od_ref[0, :, h * LANES:(h + 1) * LANES].astype(_F32), gsub_ref[...])
          * (1.0 - lam_init)).astype(_BF16) for h in range(DIFF_HEADS)], axis=1)
    mix_m = jnp.concatenate(
        [_dot(om_ref[0, :, h * LANES:(h + 1) * LANES], wv_ref[h]).astype(_BF16)
         for h in range(MLA_HEADS)], axis=1)
    x1 = x_ref[0] + _dot(mix_d, w_od_ref[...]) + _dot(mix_m, w_om_ref[...])
    h2 = _rms(x1, g_ffn_ref[...]).astype(_BF16)
    for c in range(D_FF // FF_CHUNK):
        cs = slice(c * FF_CHUNK, (c + 1) * FF_CHUNK)
        u = _dot(h2, w_up_ref[:, cs])
        g = _dot(h2, w_up_ref[:, D_FF + c * FF_CHUNK:D_FF + (c + 1) * FF_CHUNK])
        gbuf_ref[0:8, :] = carry_ref[:, cs]
        gbuf_ref[8:, :] = g
        conv = (b_conv_ref[:, cs] + gbuf_ref[6:6 + tm, :] * w_conv_ref[0:1, cs]
                + gbuf_ref[7:7 + tm, :] * w_conv_ref[1:2, cs] + g * w_conv_ref[2:3, cs])
        half = 0.5 * conv
        act_ref[:, cs] = ((half + half * jnp.tanh(half)) * u).astype(_BF16)
        carry_ref[:, cs] = g[tm - 8:, :]
    nconv_ref[0] = carry_ref[8 - (CONV_W - 1):, :]
    y_ref[0] = _rms(x1 + _dot(act_ref[...], w_down_ref[...]), g_fin_ref[...])


def _ffn_call(x, o_d, o_m, conv_prev, g_sub, wv, w_od, w_om, g_ffn, w_up, w_conv, b_conv,
              w_down, g_fin, *, tm, lam_init):
    B, T, _ = x.shape
    row = lambda c: pl.BlockSpec((1, tm, c), lambda b, t: (b, t, 0))
    full = lambda a: pl.BlockSpec(a.shape, lambda b, t: (0,) * a.ndim,
                                  pipeline_mode=pl.Buffered(1))
    state = pl.BlockSpec((1, CONV_W - 1, D_FF), lambda b, t: (b, 0, 0))
    weights = (g_sub, wv, w_od, w_om, g_ffn, w_up, w_conv, b_conv, w_down, g_fin)
    return pl.pallas_call(
        functools.partial(_ffn_kernel, tm=tm, lam_init=lam_init),
        grid=(B, T // tm),
        in_specs=[row(D_MODEL), row(DIFF_WIDTH), row(MLA_WIDTH), state]
                 + [full(w) for w in weights],
        out_specs=(row(D_MODEL), state),
        out_shape=(jax.ShapeDtypeStruct((B, T, D_MODEL), _F32),
                   jax.ShapeDtypeStruct((B, CONV_W - 1, D_FF), _F32)),
        scratch_shapes=[pltpu.VMEM((8, D_FF), _F32), pltpu.VMEM((tm + 8, FF_CHUNK), _F32),
                        pltpu.VMEM((tm, D_FF), _BF16)],
        compiler_params=pltpu.CompilerParams(
            dimension_semantics=("arbitrary", "arbitrary"), vmem_limit_bytes=VMEM_LIMIT),
        name="ffn",
    )(x, o_d, o_m, conv_prev, *weights)


def _rope_tables(pos):
    half = DIFF_HEAD_DIM // 2
    inv = ROPE_THETA ** (-jnp.arange(half, dtype=_F32) * (2.0 / DIFF_HEAD_DIM))
    ang = pos.astype(_F32)[:, None] * inv[None, :]
    cos, sin, zero = jnp.cos(ang), jnp.sin(ang), jnp.zeros_like(ang)
    cos2 = jnp.concatenate([cos, cos, cos, cos], axis=1)
    sin_lo = jnp.concatenate([-sin, zero, -sin, zero], axis=1)
    sin_hi = jnp.concatenate([zero, sin, zero, sin], axis=1)
    return cos2, sin_lo, sin_hi


def _tiles(T, n_keys):
    if T >= 512:
        return dict(tm=512, diff=(512, 512), mla=(256, 512))
    tk = -(-n_keys // LANES) * LANES
    return dict(tm=T, diff=(T, tk), mla=(T, tk))


def _layer(x, pos, past, lam_init, wl, g_final):
    (g_attn, w_in_p, lam_p, g_sub, g_qa, w_qn, w_qr, wkt, wv, g_kva, w_od, w_om, g_ffn, w_up,
     w_conv, b_conv, w_down) = wl
    B, T, _ = x.shape
    n_past = 0 if past is None else past[0].shape[1]
    n_keys, q_pos0 = n_past + T, n_past
    tl = _tiles(T, n_keys)
    tabs = _rope_tables(pos)
    qd, kd32, kd16, vd32, vd16, ckv32, kr32, kmla, vmla, qmla = _proj_call(
        x, tabs, g_attn, w_in_p, g_qa, w_qn, w_qr, wkt, g_kva, tm=tl["tm"], q_pos0=q_pos0)

    if past is None:
        kd_all, vd_all, kmla_all, vmla_all = kd16, vd16, kmla, vmla
        conv_prev = jnp.zeros((B, CONV_W - 1, D_FF), _F32)
    else:
        p_dk, p_dv, p_ckv, p_kr, conv_prev = past
        n_pad = tl["diff"][1] - n_keys
        lane = jnp.arange(LANES)[None, :]
        k_cols = jnp.where(((jnp.arange(n_past) // CHUNK)[:, None] > lane) & (lane < MASK_COLS),
                           NEG_INF, 0.0).astype(_BF16)
        k_cols = jnp.broadcast_to(k_cols, (B, n_past, LANES))
        pad_cols = jnp.where(lane < MASK_COLS, NEG_INF, 0.0).astype(_BF16)
        one = jnp.ones((B, n_past, LANES), _BF16)
        p_dk = p_dk.reshape(B, n_past, DIFF_WIDTH).astype(_BF16)
        p_dv = p_dv.reshape(B, n_past, DIFF_WIDTH).astype(_BF16)
        heads = lambda a: [a[:, :, h * LANES:(h + 1) * LANES] for h in range(DIFF_HEADS)]

        def cat(old, new, pad_row):
            pad = jnp.broadcast_to(pad_row, (B, n_pad, new.shape[-1]))
            return jnp.concatenate([jnp.concatenate(old, axis=-1), new, pad], axis=1)

        zero = lambda w: jnp.zeros((1, 1, w), _BF16)
        kd_all = cat([p for a in heads(p_dk) for p in (a, k_cols)], kd16,
                     jnp.tile(jnp.concatenate([zero(LANES)[0], pad_cols], axis=-1), DIFF_HEADS))
        vd_all = cat([p for a in heads(p_dv) for p in (a, one)], vd16, zero(DIFF_HEADS * V_ONES))
        kmla_all = cat([p_ckv.astype(_BF16), k_cols[:, :, :MASK_COLS], p_kr.astype(_BF16)], kmla,
                       jnp.concatenate([zero(LANES)[0], pad_cols[:, :MASK_COLS],
                                        zero(MLA_ROPE)[0]], -1))
        vmla_all = cat([p_ckv.astype(_BF16), one], vmla, zero(V_ONES))

    o_d = _diff_attn_call(qd, kd_all, vd_all, lam_p, tq=tl["diff"][0], tk=tl["diff"][1],
                          q_pos0=q_pos0, n_keys=n_keys, lam_init=lam_init)
    o_m = _mla_attn_call(qmla, kmla_all, vmla_all, tq=tl["mla"][0], tk=tl["mla"][1],
                         q_pos0=q_pos0, n_keys=n_keys)
    y, new_conv = _ffn_call(x, o_d, o_m, conv_prev, g_sub, wv, w_od, w_om, g_ffn, w_up, w_conv,
                            b_conv, w_down, g_final, tm=tl["tm"], lam_init=lam_init)
    state = (kd32.reshape(1, B, T, DIFF_HEADS, 2, DIFF_HEAD_DIM),
             vd32[None],
             ckv32[None], kr32[None], new_conv[None])
    return y, state


def kernel(x_prompt, x_sample, cache_diff_k, cache_diff_v, cache_mla_ckv, cache_mla_krope,
           state_conv, g_attn, w_in, lambda_q1, lambda_k1, lambda_q2, lambda_k2, g_diff_sub,
           g_q_lora, w_q_b, g_kv_lora, w_kv_b, w_out, g_ffn, w_up, w_conv, b_conv, w_down,
           g_final):
    assert g_attn.shape[0] == 1, "single-layer model"
    S = x_prompt.shape[1]
    T = x_sample.shape[1]
    P = cache_diff_k.shape[2]
    lam_init = 0.8 - 0.6 * math.exp(-0.3 * 0)

    w_in_p = jnp.concatenate([w_in[0], w_in[0][:, OFF_KR:IN_COLS]], axis=1).astype(_BF16)
    wq = w_q_b[0].reshape(MLA_Q_LORA, MLA_HEADS, MLA_NOPE + MLA_ROPE)
    w_qn = wq[:, :, :MLA_NOPE].reshape(MLA_Q_LORA, MLA_HEADS * MLA_NOPE).astype(_BF16)
    w_qr = wq[:, :, MLA_NOPE:].reshape(MLA_Q_LORA, MLA_HEADS * MLA_ROPE).astype(_BF16)
    wkv = w_kv_b[0].reshape(MLA_KV_LORA, MLA_HEADS, MLA_NOPE + MLA_V)
    wkt = jnp.transpose(wkv[:, :, :MLA_NOPE], (1, 2, 0)).astype(_BF16)
    wv = jnp.transpose(wkv[:, :, MLA_NOPE:], (1, 0, 2)).astype(_BF16)
    lam_p = jnp.concatenate([lambda_q1, lambda_k1, lambda_q2, lambda_k2], axis=0)
    wl = (g_attn, w_in_p, lam_p, g_diff_sub, g_q_lora, w_qn, w_qr, wkt, wv, g_kv_lora,
          w_out[0][:DIFF_WIDTH].astype(_BF16), w_out[0][DIFF_WIDTH:].astype(_BF16), g_ffn,
          w_up[0].astype(_BF16), w_conv[0], b_conv, w_down[0].astype(_BF16))
    g_fin = g_final[None]

    pos_p = jnp.arange(S, dtype=jnp.int32)
    pos_s = P + jnp.arange(T, dtype=jnp.int32)
    y_p, st_p = _layer(x_prompt, pos_p, None, lam_init, wl, g_fin)
    past = (cache_diff_k[0], cache_diff_v[0], cache_mla_ckv[0], cache_mla_krope[0], state_conv[0])
    y_s, st_s = _layer(x_sample, pos_s, past, lam_init, wl, g_fin)
    return (y_p, y_s) + st_p + st_s
```

```python
import functools
import math

import jax
import jax.numpy as jnp
import numpy as np
from jax import lax
from jax.experimental import pallas as pl
from jax.experimental.pallas import tpu as pltpu

D_MODEL = 1024
CHUNK = 64
ROPE_THETA = 10000.0
NORM_EPS = 1e-6
NEG_INF = -1e30

DIFF_HEADS = 4
DIFF_HEAD_DIM = 64
DIFF_WIDTH = DIFF_HEADS * 2 * DIFF_HEAD_DIM
MLA_HEADS = 4
MLA_Q_LORA = 256
MLA_KV_LORA = 128
MLA_NOPE = 128
MLA_ROPE = 64
MLA_V = 128
MLA_WIDTH = MLA_HEADS * MLA_V
MLA_SCALE = (MLA_NOPE + MLA_ROPE) ** -0.5
OFF_DQ = 0
OFF_DK = OFF_DQ + DIFF_WIDTH
OFF_DV = OFF_DK + DIFF_WIDTH
OFF_CQ = OFF_DV + DIFF_WIDTH
OFF_CKV = OFF_CQ + MLA_Q_LORA
OFF_KR = OFF_CKV + MLA_KV_LORA
IN_COLS = OFF_KR + MLA_ROPE
IN_COLS_PAD = IN_COLS + MLA_ROPE
D_FF = 2816
CONV_W = 3

LANES = 128
MASK_COLS = 64
QK_W = 2 * LANES
MLA_QK = 2 * LANES
V_ONES = 2 * LANES
FF_CHUNK = 256
TRIP_SIZES = (2, 4, 6, 12)
LOG2E = math.log2(math.e)
VMEM_LIMIT = 56 * 1024 * 1024

_BF16 = jnp.bfloat16
_F32 = jnp.float32


def _dot(a, b):
    return jnp.dot(a, b, preferred_element_type=_F32)


def _dot_t(a, b):
    return lax.dot_general(a, b, (((1,), (1,)), ((), ())), preferred_element_type=_F32)


def _rms(xf, g):
    return xf * lax.rsqrt(jnp.mean(xf * xf, axis=-1, keepdims=True) + NORM_EPS) * g


def _rope_slab(xs, cos2, sin_lo, sin_hi):
    return (xs * cos2 + pltpu.roll(xs, LANES - 32, axis=1) * sin_lo
            + pltpu.roll(xs, 32, axis=1) * sin_hi)


def _chunk_mask_cols(pos):
    lane = lax.broadcasted_iota(jnp.int32, (1, LANES), 1)
    chunk = lax.shift_right_logical(pos, CHUNK.bit_length() - 1)
    q_cols = jnp.where(lane == chunk, 1.0, 0.0)
    k_cols = jnp.where((chunk > lane) & (lane < MASK_COLS), NEG_INF, 0.0)
    return q_cols, k_cols


def _proj_kernel(x_ref, cos_ref, slo_ref, shi_ref, g_attn_ref, w_in_ref, g_qa_ref, w_qn_ref,
                 w_qr_ref, wkt_ref, g_kva_ref,
                 qd_ref, kd32_ref, kd16_ref, vd32_ref, vd16_ref, ckv32_ref, kr32_ref,
                 kmla_ref, vmla_ref, qmla_ref, *, tm, q_pos0):
    x = x_ref[0]
    h = _rms(x, g_attn_ref[...]).astype(_BF16)
    cos2, slo, shi = cos_ref[...], slo_ref[...], shi_ref[...]
    rope = functools.partial(_rope_slab, cos2=cos2, sin_lo=slo, sin_hi=shi)
    ones = jnp.ones((tm, LANES), _BF16)
    pos = q_pos0 + pl.program_id(1) * tm + lax.broadcasted_iota(jnp.int32, (tm, 1), 0)
    q_cols, k_cols = _chunk_mask_cols(pos)
    lane = lax.broadcasted_iota(jnp.int32, (1, LANES), 1)

    lat = _dot(h, w_in_ref[:, OFF_CQ:IN_COLS_PAD])
    c_q = _rms(lat[:, :MLA_Q_LORA], g_qa_ref[...]).astype(_BF16)
    c_kv = _rms(lat[:, MLA_Q_LORA:OFF_KR - OFF_CQ], g_kva_ref[...])
    kr_slab = rope(lat[:, OFF_KR - OFF_CQ:])

    proj = _dot(h, w_in_ref[:, :OFF_CQ])
    q_scale = DIFF_HEAD_DIM ** -0.5 * LOG2E
    for s in range(DIFF_HEADS):
        cs = slice(s * LANES, (s + 1) * LANES)
        qd_ref[0, :, s * QK_W:s * QK_W + LANES] = (
            rope(proj[:, OFF_DQ + s * LANES:OFF_DQ + (s + 1) * LANES]) * q_scale).astype(_BF16)
        qd_ref[0, :, s * QK_W + LANES:(s + 1) * QK_W] = q_cols.astype(_BF16)
        k = rope(proj[:, OFF_DK + s * LANES:OFF_DK + (s + 1) * LANES])
        kd32_ref[0, :, cs] = k
        kd16_ref[0, :, s * QK_W:s * QK_W + LANES] = k.astype(_BF16)
        kd16_ref[0, :, s * QK_W + LANES:(s + 1) * QK_W] = k_cols.astype(_BF16)
        v = proj[:, OFF_DV + s * LANES:OFF_DV + (s + 1) * LANES]
        vd16_ref[0, :, s * V_ONES:s * V_ONES + LANES] = v.astype(_BF16)
        vd16_ref[0, :, s * V_ONES + LANES:(s + 1) * V_ONES] = ones

    vd32_ref[0] = proj[:, OFF_DV:OFF_CQ].reshape(tm, DIFF_HEADS, LANES)
    ckv32_ref[0] = c_kv
    kr32_ref[0] = kr_slab[:, :MLA_ROPE]
    kmla_ref[0, :, :LANES] = c_kv.astype(_BF16)
    kmla_ref[0, :, LANES:] = jnp.where(lane < MASK_COLS, k_cols, kr_slab).astype(_BF16)
    vmla_ref[0, :, :LANES] = c_kv.astype(_BF16)
    vmla_ref[0, :, LANES:] = ones

    m_scale = MLA_SCALE * LOG2E
    q_nope = _dot(c_q, w_qn_ref[...]).astype(_BF16)
    q_rope = _dot(c_q, w_qr_ref[...])
    for hd in range(MLA_HEADS):
        q_lat = _dot(q_nope[:, hd * MLA_NOPE:(hd + 1) * MLA_NOPE], wkt_ref[hd])
        qmla_ref[0, hd, :, :LANES] = (q_lat * m_scale).astype(_BF16)
        if hd % 2 == 0:
            pair = rope(q_rope[:, (hd // 2) * LANES:(hd // 2 + 1) * LANES]) * m_scale
        own = pltpu.roll(pair, MLA_ROPE, axis=1) if hd % 2 == 0 else pair
        qmla_ref[0, hd, :, LANES:] = jnp.where(lane < MASK_COLS, q_cols, own).astype(_BF16)


def _proj_call(x, tabs, g_attn, w_in_p, g_qa, w_qn, w_qr, wkt, g_kva, *, tm, q_pos0):
    B, T, _ = x.shape
    nt = T // tm
    assert -(-(q_pos0 + T) // CHUNK) <= MASK_COLS, "chunk one-hot must fit the mask columns"
    row = lambda *c: pl.BlockSpec((1, tm) + c, lambda b, t: (b, t) + (0,) * len(c))
    tab = pl.BlockSpec((tm, LANES), lambda b, t: (t, 0))
    full = lambda a: pl.BlockSpec(a.shape, lambda b, t: (0,) * a.ndim)
    outs = (
        ((DIFF_HEADS * QK_W,), _BF16),
        ((DIFF_WIDTH,), _F32),
        ((DIFF_HEADS * QK_W,), _BF16),
        ((DIFF_HEADS, LANES), _F32),
        ((DIFF_HEADS * V_ONES,), _BF16),
        ((MLA_KV_LORA,), _F32),
        ((MLA_ROPE,), _F32),
        ((MLA_QK,), _BF16),
        ((V_ONES,), _BF16),
    )
    out_specs = tuple(row(*c) for c, _ in outs) + (
        pl.BlockSpec((1, MLA_HEADS, tm, MLA_QK), lambda b, t: (b, 0, t, 0)),)
    out_shape = tuple(jax.ShapeDtypeStruct((B, T) + c, d) for c, d in outs) + (
        jax.ShapeDtypeStruct((B, MLA_HEADS, T, MLA_QK), _BF16),)
    return pl.pallas_call(
        functools.partial(_proj_kernel, tm=tm, q_pos0=q_pos0),
        grid=(B, nt),
        in_specs=[row(D_MODEL), tab, tab, tab, full(g_attn), full(w_in_p), full(g_qa),
                  full(w_qn), full(w_qr), full(wkt), full(g_kva)],
        out_specs=out_specs,
        out_shape=out_shape,
        compiler_params=pltpu.CompilerParams(
            dimension_semantics=("arbitrary", "arbitrary"), vmem_limit_bytes=VMEM_LIMIT),
        name="proj",
    )(x, *tabs, g_attn, w_in_p, g_qa, w_qn, w_qr, wkt, g_kva)


def _attn_schedule(T, tq, tk, q_pos0, n_keys):
    pairs = []
    for qi in range(T // tq):
        last_pos = q_pos0 + (qi + 1) * tq - 1
        n_kv = -(-min((last_pos // CHUNK + 1) * CHUNK, n_keys) // tk)
        pairs += [(qi, kj, n_kv) for kj in range(n_kv)]
    per_trip = min(TRIP_SIZES, key=lambda n: (-len(pairs) % n, -n))
    items = [(qi, kj, int(kj == 0), int(kj == n_kv - 1), qi % per_trip) for qi, kj, n_kv in pairs]
    trips = -(-len(items) // per_trip)
    items += [(0, 0, 1, 0, per_trip)] * (per_trip * trips + 1 - len(items))
    return np.asarray(items, np.int32).T.copy(), trips, per_trip


def _flash(tab_ref, load_q, k_ref, v_ref, emit, scratch, *, trips, first_step):
    (s0, s1, m0, m1, a0, a1, acc_ref) = scratch
    tk = s0.shape[1]

    def score(w, s_ref, m_ref, a_ref, m_prev_ref):
        k0 = pl.multiple_of(tab_ref[1, w] * tk, tk)
        s = _dot_t(load_q(tab_ref[0, w]), k_ref[0, pl.ds(k0, tk), :])
        m_prev = jnp.where(tab_ref[2, w] > 0, NEG_INF, m_prev_ref[...])
        m_new = jnp.maximum(m_prev, jnp.max(s, axis=1, keepdims=True))
        s_ref[...] = s
        a_ref[...] = jnp.exp2(m_prev - m_new)
        m_ref[...] = m_new

    def finish(w, s_ref, m_ref, a_ref):
        k0 = pl.multiple_of(tab_ref[1, w] * tk, tk)
        slot = tab_ref[4, w]
        p = jnp.exp2(s_ref[...] - jnp.tile(m_ref[...], (1, tk // LANES)))
        pv = _dot(p.astype(_BF16), v_ref[0, pl.ds(k0, tk), :])
        acc_ref[slot] = jnp.tile(a_ref[...], (1, V_ONES // LANES)) * acc_ref[slot] + pv

    def emit_if_last(w):
        @pl.when(tab_ref[3, w] > 0)
        def _():
            acc = acc_ref[tab_ref[4, w]]
            emit(tab_ref[0, w], acc[:, :LANES] / acc[:, LANES:])

    @pl.when(first_step)
    def _():
        acc_ref[...] = jnp.zeros(acc_ref.shape, _F32)

    m1[...] = jnp.full(m1.shape, NEG_INF, _F32)
    score(0, s0, m0, a0, m1)

    even, odd = (s0, m0, a0), (s1, m1, a1)
    per_trip = acc_ref.shape[0] - 1

    def trip(t, carry):
        w = per_trip * t
        for i in range(0, per_trip, 2):
            score(w + i + 1, *odd, m0)
            finish(w + i, *even)
            score(w + i + 2, *even, m1)
            finish(w + i + 1, *odd)
        for i in range(per_trip):
            emit_if_last(w + i)
        return carry

    lax.fori_loop(0, trips, trip, 0)


def _attn_scratch(rows, tk, per_trip):
    s = pltpu.VMEM((rows, tk), _F32)
    r = pltpu.VMEM((rows, LANES), _F32)
    return [s, s, r, r, r, r, pltpu.VMEM((per_trip + 1, rows, V_ONES), _F32)]


def _diff_attn_kernel(tab_ref, q_ref, k_ref, v_ref, lam_ref, o_ref, *scratch, tq, trips,
                      lam_init):
    lane = lax.broadcasted_iota(jnp.int32, (1, QK_W), 1)
    lp = lam_ref[...]
    lam = (jnp.exp(jnp.sum(lp[0:1] * lp[1:2], axis=1, keepdims=True))
           - jnp.exp(jnp.sum(lp[2:3] * lp[3:4], axis=1, keepdims=True)) + lam_init)

    def load_q(qi):
        q = q_ref[0, pl.ds(pl.multiple_of(qi * tq, tq), tq), :]
        zero = jnp.zeros_like(q)
        return jnp.concatenate(
            [jnp.where((lane < DIFF_HEAD_DIM) | (lane >= LANES), q, zero),
             jnp.where(lane >= DIFF_HEAD_DIM, q, zero)], axis=0)

    def emit(qi, o):
        o_ref[0, pl.ds(pl.multiple_of(qi * tq, tq), tq), :] = (
            o[:tq] - lam * o[tq:]).astype(o_ref.dtype)

    first_step = (pl.program_id(0) == 0) & (pl.program_id(1) == 0)
    _flash(tab_ref, load_q, k_ref, v_ref, emit, scratch, trips=trips, first_step=first_step)


def _diff_attn_call(qd, kd, vd, lam_p, *, tq, tk, q_pos0, n_keys, lam_init):
    B, T, _ = qd.shape
    Sk = kd.shape[1]
    tab, trips, per_trip = _attn_schedule(T, tq, tk, q_pos0, n_keys)
    kern = functools.partial(_diff_attn_kernel, tq=tq, trips=trips, lam_init=lam_init)
    head = lambda rows, c: pl.BlockSpec((1, rows, c), lambda b, h, tab: (b, 0, h))
    return pl.pallas_call(
        kern,
        grid_spec=pltpu.PrefetchScalarGridSpec(
            num_scalar_prefetch=1,
            grid=(B, DIFF_HEADS),
            in_specs=[head(T, QK_W), head(Sk, QK_W), head(Sk, V_ONES),
                      pl.BlockSpec(lam_p.shape, lambda b, h, tab: (0, 0))],
            out_specs=head(T, LANES),
            scratch_shapes=_attn_scratch(2 * tq, tk, per_trip)),
        out_shape=jax.ShapeDtypeStruct((B, T, DIFF_WIDTH), _BF16),
        compiler_params=pltpu.CompilerParams(
            dimension_semantics=("arbitrary",) * 2, vmem_limit_bytes=VMEM_LIMIT),
        name="diff_attn",
    )(jnp.asarray(tab), qd, kd, vd, lam_p)


def _mla_attn_kernel(tab_ref, q_ref, k_ref, v_ref, o_ref, *scratch, tq, trips):
    def load_q(qi):
        q = q_ref[0, :, pl.ds(pl.multiple_of(qi * tq, tq), tq), :]
        return q.reshape(MLA_HEADS * tq, MLA_QK)

    def emit(qi, o):
        o = o.astype(o_ref.dtype)
        for h in range(MLA_HEADS):
            o_ref[0, pl.ds(pl.multiple_of(qi * tq, tq), tq), h * LANES:(h + 1) * LANES] = (
                o[h * tq:(h + 1) * tq])

    _flash(tab_ref, load_q, k_ref, v_ref, emit, scratch, trips=trips,
           first_step=pl.program_id(0) == 0)


def _mla_attn_call(qm, kk, vv, *, tq, tk, q_pos0, n_keys):
    B, _, T, _ = qm.shape
    Sk = kk.shape[1]
    tab, trips, per_trip = _attn_schedule(T, tq, tk, q_pos0, n_keys)
    kern = functools.partial(_mla_attn_kernel, tq=tq, trips=trips)
    kv = pl.BlockSpec((1, Sk, MLA_QK), lambda b, tab: (b, 0, 0))
    return pl.pallas_call(
        kern,
        grid_spec=pltpu.PrefetchScalarGridSpec(
            num_scalar_prefetch=1,
            grid=(B,),
            in_specs=[pl.BlockSpec((1, MLA_HEADS, T, MLA_QK), lambda b, tab: (b, 0, 0, 0)),
                      kv, kv],
            out_specs=pl.BlockSpec((1, T, MLA_HEADS * LANES), lambda b, tab: (b, 0, 0)),
            scratch_shapes=_attn_scratch(MLA_HEADS * tq, tk, per_trip)),
        out_shape=jax.ShapeDtypeStruct((B, T, MLA_HEADS * LANES), _BF16),
        compiler_params=pltpu.CompilerParams(
            dimension_semantics=("arbitrary",), vmem_limit_bytes=VMEM_LIMIT),
        name="mla_attn",
    )(jnp.asarray(tab), qm, kk, vv)


def _ffn_kernel(x_ref, od_ref, om_ref, cprev_ref, gsub_ref, wv_ref, w_od_ref, w_om_ref,
                g_ffn_ref, w_up_ref, w_conv_ref, b_conv_ref, w_down_ref, g_fin_ref,
                y_ref, nconv_ref, carry_ref, gbuf_ref, act_ref, *, tm, lam_init):
    t = pl.program_id(1)

    @pl.when(t == 0)
    def _():
        carry_ref[...] = jnp.zeros(carry_ref.shape, _F32)
        carry_ref[8 - (CONV_W - 1):, :] = cprev_ref[0]

    mix_d = jnp.concatenate(
        [(_rms(od_ref[0, :, h * LANES:(h + 1) * LANES].astype(_F32), gsub_ref[...])
          * (1.0 - lam_init)).astype(_BF16) for h in range(DIFF_HEADS)], axis=1)
    mix_m = jnp.concatenate(
        [_dot(om_ref[0, :, h * LANES:(h + 1) * LANES], wv_ref[h]).astype(_BF16)
         for h in range(MLA_HEADS)], axis=1)
    x1 = x_ref[0] + _dot(mix_d, w_od_ref[...]) + _dot(mix_m, w_om_ref[...])
    h2 = _rms(x1, g_ffn_ref[...]).astype(_BF16)
    for c in range(D_FF // FF_CHUNK):
        cs = slice(c * FF_CHUNK, (c + 1) * FF_CHUNK)
        u = _dot(h2, w_up_ref[:, cs])
        g = _dot(h2, w_up_ref[:, D_FF + c * FF_CHUNK:D_FF + (c + 1) * FF_CHUNK])
        gbuf_ref[0:8, :] = carry_ref[:, cs]
        gbuf_ref[8:, :] = g
        conv = (b_conv_ref[:, cs] + gbuf_ref[6:6 + tm, :] * w_conv_ref[0:1, cs]
                + gbuf_ref[7:7 + tm, :] * w_conv_ref[1:2, cs] + g * w_conv_ref[2:3, cs])
        half = 0.5 * conv
        act_ref[:, cs] = ((half + half * jnp.tanh(half)) * u).astype(_BF16)
        carry_ref[:, cs] = g[tm - 8:, :]
    nconv_ref[0] = carry_ref[8 - (CONV_W - 1):, :]
    y_ref[0] = _rms(x1 + _dot(act_ref[...], w_down_ref[...]), g_fin_ref[...])


def _ffn_call(x, o_d, o_m, conv_prev, g_sub, wv, w_od, w_om, g_ffn, w_up, w_conv, b_conv,
              w_down, g_fin, *, tm, lam_init):
    B, T, _ = x.shape
    row = lambda c: pl.BlockSpec((1, tm, c), lambda b, t: (b, t, 0))
    full = lambda a: pl.BlockSpec(a.shape, lambda b, t: (0,) * a.ndim,
                                  pipeline_mode=pl.Buffered(1))
    state = pl.BlockSpec((1, CONV_W - 1, D_FF), lambda b, t: (b, 0, 0))
    weights = (g_sub, wv, w_od, w_om, g_ffn, w_up, w_conv, b_conv, w_down, g_fin)
    return pl.pallas_call(
        functools.partial(_ffn_kernel, tm=tm, lam_init=lam_init),
        grid=(B, T // tm),
        in_specs=[row(D_MODEL), row(DIFF_WIDTH), row(MLA_WIDTH), state]
                 + [full(w) for w in weights],
        out_specs=(row(D_MODEL), state),
        out_shape=(jax.ShapeDtypeStruct((B, T, D_MODEL), _F32),
                   jax.ShapeDtypeStruct((B, CONV_W - 1, D_FF), _F32)),
        scratch_shapes=[pltpu.VMEM((8, D_FF), _F32), pltpu.VMEM((tm + 8, FF_CHUNK), _F32),
                        pltpu.VMEM((tm, D_FF), _BF16)],
        compiler_params=pltpu.CompilerParams(
            dimension_semantics=("arbitrary", "arbitrary"), vmem_limit_bytes=VMEM_LIMIT),
        name="ffn",
    )(x, o_d, o_m, conv_prev, *weights)


def _rope_tables(pos):
    half = DIFF_HEAD_DIM // 2
    inv = ROPE_THETA ** (-jnp.arange(half, dtype=_F32) * (2.0 / DIFF_HEAD_DIM))
    ang = pos.astype(_F32)[:, None] * inv[None, :]
    cos, sin, zero = jnp.cos(ang), jnp.sin(ang), jnp.zeros_like(ang)
    cos2 = jnp.concatenate([cos, cos, cos, cos], axis=1)
    sin_lo = jnp.concatenate([-sin, zero, -sin, zero], axis=1)
    sin_hi = jnp.concatenate([zero, sin, zero, sin], axis=1)
    return cos2, sin_lo, sin_hi


def _tiles(T, n_keys):
    if T >= 512:
        return dict(tm=512, diff=(512, 512), mla=(256, 512))
    tk = -(-n_keys // LANES) * LANES
    return dict(tm=T, diff=(T, tk), mla=(T, tk))


def _layer(x, pos, past, lam_init, wl, g_final):
    (g_attn, w_in_p, lam_p, g_sub, g_qa, w_qn, w_qr, wkt, wv, g_kva, w_od, w_om, g_ffn, w_up,
     w_conv, b_conv, w_down) = wl
    B, T, _ = x.shape
    n_past = 0 if past is None else past[0].shape[1]
    n_keys, q_pos0 = n_past + T, n_past
    tl = _tiles(T, n_keys)
    tabs = _rope_tables(pos)
    qd, kd32, kd16, vd32, vd16, ckv32, kr32, kmla, vmla, qmla = _proj_call(
        x, tabs, g_attn, w_in_p, g_qa, w_qn, w_qr, wkt, g_kva, tm=tl["tm"], q_pos0=q_pos0)

    if past is None:
        kd_all, vd_all, kmla_all, vmla_all = kd16, vd16, kmla, vmla
        conv_prev = jnp.zeros((B, CONV_W - 1, D_FF), _F32)
    else:
        p_dk, p_dv, p_ckv, p_kr, conv_prev = past
        n_pad = tl["diff"][1] - n_keys
        lane = jnp.arange(LANES)[None, :]
        k_cols = jnp.where(((jnp.arange(n_past) // CHUNK)[:, None] > lane) & (lane < MASK_COLS),
                           NEG_INF, 0.0).astype(_BF16)
        k_cols = jnp.broadcast_to(k_cols, (B, n_past, LANES))
        pad_cols = jnp.where(lane < MASK_COLS, NEG_INF, 0.0).astype(_BF16)
        one = jnp.ones((B, n_past, LANES), _BF16)
        p_dk = p_dk.reshape(B, n_past, DIFF_WIDTH).astype(_BF16)
        p_dv = p_dv.reshape(B, n_past, DIFF_WIDTH).astype(_BF16)
        heads = lambda a: [a[:, :, h * LANES:(h + 1) * LANES] for h in range(DIFF_HEADS)]

        def cat(old, new, pad_row):
            pad = jnp.broadcast_to(pad_row, (B, n_pad, new.shape[-1]))
            return jnp.concatenate([jnp.concatenate(old, axis=-1), new, pad], axis=1)

        zero = lambda w: jnp.zeros((1, 1, w), _BF16)
        kd_all = cat([p for a in heads(p_dk) for p in (a, k_cols)], kd16,
                     jnp.tile(jnp.concatenate([zero(LANES)[0], pad_cols], axis=-1), DIFF_HEADS))
        vd_all = cat([p for a in heads(p_dv) for p in (a, one)], vd16, zero(DIFF_HEADS * V_ONES))
        kmla_all = cat([p_ckv.astype(_BF16), k_cols[:, :, :MASK_COLS], p_kr.astype(_BF16)], kmla,
                       jnp.concatenate([zero(LANES)[0], pad_cols[:, :MASK_COLS],
                                        zero(MLA_ROPE)[0]], -1))
        vmla_all = cat([p_ckv.astype(_BF16), one], vmla, zero(V_ONES))

    o_d = _diff_attn_call(qd, kd_all, vd_all, lam_p, tq=tl["diff"][0], tk=tl["diff"][1],
                          q_pos0=q_pos0, n_keys=n_keys, lam_init=lam_init)
    o_m = _mla_attn_call(qmla, kmla_all, vmla_all, tq=tl["mla"][0], tk=tl["mla"][1],
                         q_pos0=q_pos0, n_keys=n_keys)
    y, new_conv = _ffn_call(x, o_d, o_m, conv_prev, g_sub, wv, w_od, w_om, g_ffn, w_up, w_conv,
                            b_conv, w_down, g_final, tm=tl["tm"], lam_init=lam_init)
    state = (kd32.reshape(1, B, T, DIFF_HEADS, 2, DIFF_HEAD_DIM),
             vd32[None],
             ckv32[None], kr32[None], new_conv[None])
    return y, state


def kernel(x_prompt, x_sample, cache_diff_k, cache_diff_v, cache_mla_ckv, cache_mla_krope,
           state_conv, g_attn, w_in, lambda_q1, lambda_k1, lambda_q2, lambda_k2, g_diff_sub,
           g_q_lora, w_q_b, g_kv_lora, w_kv_b, w_out, g_ffn, w_up, w_conv, b_conv, w_down,
           g_final):
    assert g_attn.shape[0] == 1, "single-layer model"
    S = x_prompt.shape[1]
    T = x_sample.shape[1]
    P = cache_diff_k.shape[2]
    lam_init = 0.8 - 0.6 * math.exp(-0.3 * 0)

    w_in_p = jnp.concatenate([w_in[0], w_in[0][:, OFF_KR:IN_COLS]], axis=1).astype(_BF16)
    wq = w_q_b[0].reshape(MLA_Q_LORA, MLA_HEADS, MLA_NOPE + MLA_ROPE)
    w_qn = wq[:, :, :MLA_NOPE].reshape(MLA_Q_LORA, MLA_HEADS * MLA_NOPE).astype(_BF16)
    w_qr = wq[:, :, MLA_NOPE:].reshape(MLA_Q_LORA, MLA_HEADS * MLA_ROPE).astype(_BF16)
    wkv = w_kv_b[0].reshape(MLA_KV_LORA, MLA_HEADS, MLA_NOPE + MLA_V)
    wkt = jnp.transpose(wkv[:, :, :MLA_NOPE], (1, 2, 0)).astype(_BF16)
    wv = jnp.transpose(wkv[:, :, MLA_NOPE:], (1, 0, 2)).astype(_BF16)
    lam_p = jnp.concatenate([lambda_q1, lambda_k1, lambda_q2, lambda_k2], axis=0)
    wl = (g_attn, w_in_p, lam_p, g_diff_sub, g_q_lora, w_qn, w_qr, wkt, wv, g_kv_lora,
          w_out[0][:DIFF_WIDTH].astype(_BF16), w_out[0][DIFF_WIDTH:].astype(_BF16), g_ffn,
          w_up[0].astype(_BF16), w_conv[0], b_conv, w_down[0].astype(_BF16))
    g_fin = g_final[None]

    pos_p = jnp.arange(S, dtype=jnp.int32)
    pos_s = P + jnp.arange(T, dtype=jnp.int32)
    y_p, st_p = _layer(x_prompt, pos_p, None, lam_init, wl, g_fin)
    past = (cache_diff_k[0], cache_diff_v[0], cache_mla_ckv[0], cache_mla_krope[0], state_conv[0])
    y_s, st_s = _layer(x_sample, pos_s, past, lam_init, wl, g_fin)
    return (y_p, y_s) + st_p + st_s
```

```python
import functools
import math

import jax
import jax.numpy as jnp
import numpy as np
from jax import lax
from jax.experimental import pallas as pl
from jax.experimental.pallas import tpu as pltpu

D_MODEL = 1024
CHUNK = 64
ROPE_THETA = 10000.0
NORM_EPS = 1e-6
NEG_INF = -1e30

DIFF_HEADS = 4
DIFF_HEAD_DIM = 64
DIFF_WIDTH = DIFF_HEADS * 2 * DIFF_HEAD_DIM
MLA_HEADS = 4
MLA_Q_LORA = 256
MLA_KV_LORA = 128
MLA_NOPE = 128
MLA_ROPE = 64
MLA_V = 128
MLA_WIDTH = MLA_HEADS * MLA_V
MLA_SCALE = (MLA_NOPE + MLA_ROPE) ** -0.5
OFF_DQ = 0
OFF_DK = OFF_DQ + DIFF_WIDTH
OFF_DV = OFF_DK + DIFF_WIDTH
OFF_CQ = OFF_DV + DIFF_WIDTH
OFF_CKV = OFF_CQ + MLA_Q_LORA
OFF_KR = OFF_CKV + MLA_KV_LORA
IN_COLS = OFF_KR + MLA_ROPE
IN_COLS_PAD = IN_COLS + MLA_ROPE
D_FF = 2816
CONV_W = 3

LANES = 128
MASK_COLS = 64
QK_W = 2 * LANES
MLA_QK = 2 * LANES
V_ONES = 2 * LANES
FF_CHUNK = 256
TRIP_SIZES = (2, 4, 6, 12)
LOG2E = math.log2(math.e)
VMEM_LIMIT = 56 * 1024 * 1024

_BF16 = jnp.bfloat16
_F32 = jnp.float32


def _dot(a, b):
    return jnp.dot(a, b, preferred_element_type=_F32)


def _dot_t(a, b):
    return lax.dot_general(a, b, (((1,), (1,)), ((), ())), preferred_element_type=_F32)


def _rms(xf, g):
    return xf * lax.rsqrt(jnp.mean(xf * xf, axis=-1, keepdims=True) + NORM_EPS) * g


def _rope_slab(xs, cos2, sin_lo, sin_hi):
    return (xs * cos2 + pltpu.roll(xs, LANES - 32, axis=1) * sin_lo
            + pltpu.roll(xs, 32, axis=1) * sin_hi)


def _chunk_mask_cols(pos):
    lane = lax.broadcasted_iota(jnp.int32, (1, LANES), 1)
    chunk = lax.shift_right_logical(pos, CHUNK.bit_length() - 1)
    q_cols = jnp.where(lane == chunk, 1.0, 0.0)
    k_cols = jnp.where((chunk > lane) & (lane < MASK_COLS), NEG_INF, 0.0)
    return q_cols, k_cols


def _proj_kernel(x_ref, cos_ref, slo_ref, shi_ref, g_attn_ref, w_in_ref, g_qa_ref, w_qn_ref,
                 w_qr_ref, wkt_ref, g_kva_ref,
                 qd_ref, kd32_ref, kd16_ref, vd32_ref, vd16_ref, ckv32_ref, kr32_ref,
                 kmla_ref, vmla_ref, qmla_ref, *, tm, q_pos0):
    x = x_ref[0]
    h = _rms(x, g_attn_ref[...]).astype(_BF16)
    cos2, slo, shi = cos_ref[...], slo_ref[...], shi_ref[...]
    rope = functools.partial(_rope_slab, cos2=cos2, sin_lo=slo, sin_hi=shi)
    ones = jnp.ones((tm, LANES), _BF16)
    pos = q_pos0 + pl.program_id(1) * tm + lax.broadcasted_iota(jnp.int32, (tm, 1), 0)
    q_cols, k_cols = _chunk_mask_cols(pos)
    lane = lax.broadcasted_iota(jnp.int32, (1, LANES), 1)

    lat = _dot(h, w_in_ref[:, OFF_CQ:IN_COLS_PAD])
    c_q = _rms(lat[:, :MLA_Q_LORA], g_qa_ref[...]).astype(_BF16)
    c_kv = _rms(lat[:, MLA_Q_LORA:OFF_KR - OFF_CQ], g_kva_ref[...])
    kr_slab = rope(lat[:, OFF_KR - OFF_CQ:])

    proj = _dot(h, w_in_ref[:, :OFF_CQ])
    q_scale = DIFF_HEAD_DIM ** -0.5 * LOG2E
    for s in range(DIFF_HEADS):
        cs = slice(s * LANES, (s + 1) * LANES)
        qd_ref[0, :, s * QK_W:s * QK_W + LANES] = (
            rope(proj[:, OFF_DQ + s * LANES:OFF_DQ + (s + 1) * LANES]) * q_scale).astype(_BF16)
        qd_ref[0, :, s * QK_W + LANES:(s + 1) * QK_W] = q_cols.astype(_BF16)
        k = rope(proj[:, OFF_DK + s * LANES:OFF_DK + (s + 1) * LANES])
        kd32_ref[0, :, cs] = k
        kd16_ref[0, :, s * QK_W:s * QK_W + LANES] = k.astype(_BF16)
        kd16_ref[0, :, s * QK_W + LANES:(s + 1) * QK_W] = k_cols.astype(_BF16)
        v = proj[:, OFF_DV + s * LANES:OFF_DV + (s + 1) * LANES]
        vd16_ref[0, :, s * V_ONES:s * V_ONES + LANES] = v.astype(_BF16)
        vd16_ref[0, :, s * V_ONES + LANES:(s + 1) * V_ONES] = ones

    vd32_ref[0] = proj[:, OFF_DV:OFF_CQ].reshape(tm, DIFF_HEADS, LANES)
    ckv32_ref[0] = c_kv
    kr32_ref[0] = kr_slab[:, :MLA_ROPE]
    kmla_ref[0, :, :LANES] = c_kv.astype(_BF16)
    kmla_ref[0, :, LANES:] = jnp.where(lane < MASK_COLS, k_cols, kr_slab).astype(_BF16)
    vmla_ref[0, :, :LANES] = c_kv.astype(_BF16)
    vmla_ref[0, :, LANES:] = ones

    m_scale = MLA_SCALE * LOG2E
    q_nope = _dot(c_q, w_qn_ref[...]).astype(_BF16)
    q_rope = _dot(c_q, w_qr_ref[...])
    for hd in range(MLA_HEADS):
        q_lat = _dot(q_nope[:, hd * MLA_NOPE:(hd + 1) * MLA_NOPE], wkt_ref[hd])
        qmla_ref[0, hd, :, :LANES] = (q_lat * m_scale).astype(_BF16)
        if hd % 2 == 0:
            pair = rope(q_rope[:, (hd // 2) * LANES:(hd // 2 + 1) * LANES]) * m_scale
        own = pltpu.roll(pair, MLA_ROPE, axis=1) if hd % 2 == 0 else pair
        qmla_ref[0, hd, :, LANES:] = jnp.where(lane < MASK_COLS, q_cols, own).astype(_BF16)


def _proj_call(x, tabs, g_attn, w_in_p, g_qa, w_qn, w_qr, wkt, g_kva, *, tm, q_pos0):
    B, T, _ = x.shape
    nt = T // tm
    assert -(-(q_pos0 + T) // CHUNK) <= MASK_COLS, "chunk one-hot must fit the mask columns"
    row = lambda *c: pl.BlockSpec((1, tm) + c, lambda b, t: (b, t) + (0,) * len(c))
    tab = pl.BlockSpec((tm, LANES), lambda b, t: (t, 0))
    full = lambda a: pl.BlockSpec(a.shape, lambda b, t: (0,) * a.ndim)
    outs = (
        ((DIFF_HEADS * QK_W,), _BF16),
        ((DIFF_WIDTH,), _F32),
        ((DIFF_HEADS * QK_W,), _BF16),
        ((DIFF_HEADS, LANES), _F32),
        ((DIFF_HEADS * V_ONES,), _BF16),
        ((MLA_KV_LORA,), _F32),
        ((MLA_ROPE,), _F32),
        ((MLA_QK,), _BF16),
        ((V_ONES,), _BF16),
    )
    out_specs = tuple(row(*c) for c, _ in outs) + (
        pl.BlockSpec((1, MLA_HEADS, tm, MLA_QK), lambda b, t: (b, 0, t, 0)),)
    out_shape = tuple(jax.ShapeDtypeStruct((B, T) + c, d) for c, d in outs) + (
        jax.ShapeDtypeStruct((B, MLA_HEADS, T, MLA_QK), _BF16),)
    return pl.pallas_call(
        functools.partial(_proj_kernel, tm=tm, q_pos0=q_pos0),
        grid=(B, nt),
        in_specs=[row(D_MODEL), tab, tab, tab, full(g_attn), full(w_in_p), full(g_qa),
                  full(w_qn), full(w_qr), full(wkt), full(g_kva)],
        out_specs=out_specs,
        out_shape=out_shape,
        compiler_params=pltpu.CompilerParams(
            dimension_semantics=("arbitrary", "arbitrary"), vmem_limit_bytes=VMEM_LIMIT),
        name="proj",
    )(x, *tabs, g_attn, w_in_p, g_qa, w_qn, w_qr, wkt, g_kva)


def _attn_schedule(T, tq, tk, q_pos0, n_keys):
    pairs = []
    for qi in range(T // tq):
        last_pos = q_pos0 + (qi + 1) * tq - 1
        n_kv = -(-min((last_pos // CHUNK + 1) * CHUNK, n_keys) // tk)
        pairs += [(qi, kj, n_kv) for kj in range(n_kv)]
    per_trip = min(TRIP_SIZES, key=lambda n: (-len(pairs) % n, -n))
    items = [(qi, kj, int(kj == 0), int(kj == n_kv - 1), qi % per_trip) for qi, kj, n_kv in pairs]
    trips = -(-len(items) // per_trip)
    items += [(0, 0, 1, 0, per_trip)] * (per_trip * trips + 1 - len(items))
    return np.asarray(items, np.int32).T.copy(), trips, per_trip


def _flash(tab_ref, load_q, k_ref, v_ref, emit, scratch, *, trips, first_step):
    (s0, s1, m0, m1, a0, a1, acc_ref) = scratch
    tk = s0.shape[1]

    def score(w, s_ref, m_ref, a_ref, m_prev_ref):
        k0 = pl.multiple_of(tab_ref[1, w] * tk, tk)
        s = _dot_t(load_q(tab_ref[0, w]), k_ref[0, pl.ds(k0, tk), :])
        m_prev = jnp.where(tab_ref[2, w] > 0, NEG_INF, m_prev_ref[...])
        m_new = jnp.maximum(m_prev, jnp.max(s, axis=1, keepdims=True))
        s_ref[...] = s
        a_ref[...] = jnp.exp2(m_prev - m_new)
        m_ref[...] = m_new

    def finish(w, s_ref, m_ref, a_ref):
        k0 = pl.multiple_of(tab_ref[1, w] * tk, tk)
        slot = tab_ref[4, w]
        p = jnp.exp2(s_ref[...] - jnp.tile(m_ref[...], (1, tk // LANES)))
        pv = _dot(p.astype(_BF16), v_ref[0, pl.ds(k0, tk), :])
        acc_ref[slot] = jnp.tile(a_ref[...], (1, V_ONES // LANES)) * acc_ref[slot] + pv

    def emit_if_last(w):
        @pl.when(tab_ref[3, w] > 0)
        def _():
            acc = acc_ref[tab_ref[4, w]]
            emit(tab_ref[0, w], acc[:, :LANES] / acc[:, LANES:])

    @pl.when(first_step)
    def _():
        acc_ref[...] = jnp.zeros(acc_ref.shape, _F32)

    m1[...] = jnp.full(m1.shape, NEG_INF, _F32)
    score(0, s0, m0, a0, m1)

    even, odd = (s0, m0, a0), (s1, m1, a1)
    per_trip = acc_ref.shape[0] - 1

    def trip(t, carry):
        w = per_trip * t
        for i in range(0, per_trip, 2):
            score(w + i + 1, *odd, m0)
            finish(w + i, *even)
            score(w + i + 2, *even, m1)
            finish(w + i + 1, *odd)
        for i in range(per_trip):
            emit_if_last(w + i)
        return carry

    lax.fori_loop(0, trips, trip, 0)


def _attn_scratch(rows, tk, per_trip):
    s = pltpu.VMEM((rows, tk), _F32)
    r = pltpu.VMEM((rows, LANES), _F32)
    return [s, s, r, r, r, r, pltpu.VMEM((per_trip + 1, rows, V_ONES), _F32)]


def _diff_attn_kernel(tab_ref, q_ref, k_ref, v_ref, lam_ref, o_ref, *scratch, tq, trips,
                      lam_init):
    lane = lax.broadcasted_iota(jnp.int32, (1, QK_W), 1)
    lp = lam_ref[...]
    lam = (jnp.exp(jnp.sum(lp[0:1] * lp[1:2], axis=1, keepdims=True))
           - jnp.exp(jnp.sum(lp[2:3] * lp[3:4], axis=1, keepdims=True)) + lam_init)

    def load_q(qi):
        q = q_ref[0, pl.ds(pl.multiple_of(qi * tq, tq), tq), :]
        zero = jnp.zeros_like(q)
        return jnp.concatenate(
            [jnp.where((lane < DIFF_HEAD_DIM) | (lane >= LANES), q, zero),
             jnp.where(lane >= DIFF_HEAD_DIM, q, zero)], axis=0)

    def emit(qi, o):
        o_ref[0, pl.ds(pl.multiple_of(qi * tq, tq), tq), :] = (
            o[:tq] - lam * o[tq:]).astype(o_ref.dtype)

    first_step = (pl.program_id(0) == 0) & (pl.program_id(1) == 0)
    _flash(tab_ref, load_q, k_ref, v_ref, emit, scratch, trips=trips, first_step=first_step)


def _diff_attn_call(qd, kd, vd, lam_p, *, tq, tk, q_pos0, n_keys, lam_init):
    B, T, _ = qd.shape
    Sk = kd.shape[1]
    tab, trips, per_trip = _attn_schedule(T, tq, tk, q_pos0, n_keys)
    kern = functools.partial(_diff_attn_kernel, tq=tq, trips=trips, lam_init=lam_init)
    head = lambda rows, c: pl.BlockSpec((1, rows, c), lambda b, h, tab: (b, 0, h))
    return pl.pallas_call(
        kern,
        grid_spec=pltpu.PrefetchScalarGridSpec(
            num_scalar_prefetch=1,
            grid=(B, DIFF_HEADS),
            in_specs=[head(T, QK_W), head(Sk, QK_W), head(Sk, V_ONES),
                      pl.BlockSpec(lam_p.shape, lambda b, h, tab: (0, 0))],
            out_specs=head(T, LANES),
            scratch_shapes=_attn_scratch(2 * tq, tk, per_trip)),
        out_shape=jax.ShapeDtypeStruct((B, T, DIFF_WIDTH), _BF16),
        compiler_params=pltpu.CompilerParams(
            dimension_semantics=("arbitrary",) * 2, vmem_limit_bytes=VMEM_LIMIT),
        name="diff_attn",
    )(jnp.asarray(tab), qd, kd, vd, lam_p)


def _mla_attn_kernel(tab_ref, q_ref, k_ref, v_ref, o_ref, *scratch, tq, trips):
    def load_q(qi):
        q = q_ref[0, :, pl.ds(pl.multiple_of(qi * tq, tq), tq), :]
        return q.reshape(MLA_HEADS * tq, MLA_QK)

    def emit(qi, o):
        o = o.astype(o_ref.dtype)
        for h in range(MLA_HEADS):
            o_ref[0, pl.ds(pl.multiple_of(qi * tq, tq), tq), h * LANES:(h + 1) * LANES] = (
                o[h * tq:(h + 1) * tq])

    _flash(tab_ref, load_q, k_ref, v_ref, emit, scratch, trips=trips,
           first_step=pl.program_id(0) == 0)


def _mla_attn_call(qm, kk, vv, *, tq, tk, q_pos0, n_keys):
    B, _, T, _ = qm.shape
    Sk = kk.shape[1]
    tab, trips, per_trip = _attn_schedule(T, tq, tk, q_pos0, n_keys)
    kern = functools.partial(_mla_attn_kernel, tq=tq, trips=trips)
    kv = pl.BlockSpec((1, Sk, MLA_QK), lambda b, tab: (b, 0, 0))
    return pl.pallas_call(
        kern,
        grid_spec=pltpu.PrefetchScalarGridSpec(
            num_scalar_prefetch=1,
            grid=(B,),
            in_specs=[pl.BlockSpec((1, MLA_HEADS, T, MLA_QK), lambda b, tab: (b, 0, 0, 0)),
                      kv, kv],
            out_specs=pl.BlockSpec((1, T, MLA_HEADS * LANES), lambda b, tab: (b, 0, 0)),
            scratch_shapes=_attn_scratch(MLA_HEADS * tq, tk, per_trip)),
        out_shape=jax.ShapeDtypeStruct((B, T, MLA_HEADS * LANES), _BF16),
        compiler_params=pltpu.CompilerParams(
            dimension_semantics=("arbitrary",), vmem_limit_bytes=VMEM_LIMIT),
        name="mla_attn",
    )(jnp.asarray(tab), qm, kk, vv)


def _ffn_kernel(x_ref, od_ref, om_ref, cprev_ref, gsub_ref, wv_ref, w_od_ref, w_om_ref,
                g_ffn_ref, w_up_ref, w_conv_ref, b_conv_ref, w_down_ref, g_fin_ref,
                y_ref, nconv_ref, carry_ref, gbuf_ref, act_ref, *, tm, lam_init):
    t = pl.program_id(1)

    @pl.when(t == 0)
    def _():
        carry_ref[...] = jnp.zeros(carry_ref.shape, _F32)
        carry_ref[8 - (CONV_W - 1):, :] = cprev_ref[0]

    mix_d = jnp.concatenate(
        [(_rms(od_ref[0, :, h * LANES:(h + 1) * LANES].astype(_F32), gsub_ref[...])
          * (1.0 - lam_init)).astype(_BF16) for h in range(DIFF_HEADS)], axis=1)
    mix_m = jnp.concatenate(
        [_dot(om_ref[0, :, h * LANES:(h + 1) * LANES], wv_ref[h]).astype(_BF16)
         for h in range(MLA_HEADS)], axis=1)
    x1 = x_ref[0] + _dot(mix_d, w_od_ref[...]) + _dot(mix_m, w_om_ref[...])
    h2 = _rms(x1, g_ffn_ref[...]).astype(_BF16)
    for c in range(D_FF // FF_CHUNK):
        cs = slice(c * FF_CHUNK, (c + 1) * FF_CHUNK)
        u = _dot(h2, w_up_ref[:, cs])
        g = _dot(h2, w_up_ref[:, D_FF + c * FF_CHUNK:D_FF + (c + 1) * FF_CHUNK])
        gbuf_ref[0:8, :] = carry_ref[:, cs]
        gbuf_ref[8:, :] = g
        conv = (b_conv_ref[:, cs] + gbuf_ref[6:6 + tm, :] * w_conv_ref[0:1, cs]
                + gbuf_ref[7:7 + tm, :] * w_conv_ref[1:2, cs] + g * w_conv_ref[2:3, cs])
        half = 0.5 * conv
        act_ref[:, cs] = ((half + half * jnp.tanh(half)) * u).astype(_BF16)
        carry_ref[:, cs] = g[tm - 8:, :]
    nconv_ref[0] = carry_ref[8 - (CONV_W - 1):, :]
    y_ref[0] = _rms(x1 + _dot(act_ref[...], w_down_ref[...]), g_fin_ref[...])


def _ffn_call(x, o_d, o_m, conv_prev, g_sub, wv, w_od, w_om, g_ffn, w_up, w_conv, b_conv,
              w_down, g_fin, *, tm, lam_init):
    B, T, _ = x.shape
    row = lambda c: pl.BlockSpec((1, tm, c), lambda b, t: (b, t, 0))
    full = lambda a: pl.BlockSpec(a.shape, lambda b, t: (0,) * a.ndim,
                                  pipeline_mode=pl.Buffered(1))
    state = pl.BlockSpec((1, CONV_W - 1, D_FF), lambda b, t: (b, 0, 0))
    weights = (g_sub, wv, w_od, w_om, g_ffn, w_up, w_conv, b_conv, w_down, g_fin)
    return pl.pallas_call(
        functools.partial(_ffn_kernel, tm=tm, lam_init=lam_init),
        grid=(B, T // tm),
        in_specs=[row(D_MODEL), row(DIFF_WIDTH), row(MLA_WIDTH), state]
                 + [full(w) for w in weights],
        out_specs=(row(D_MODEL), state),
        out_shape=(jax.ShapeDtypeStruct((B, T, D_MODEL), _F32),
                   jax.ShapeDtypeStruct((B, CONV_W - 1, D_FF), _F32)),
        scratch_shapes=[pltpu.VMEM((8, D_FF), _F32), pltpu.VMEM((tm + 8, FF_CHUNK), _F32),
                        pltpu.VMEM((tm, D_FF), _BF16)],
        compiler_params=pltpu.CompilerParams(
            dimension_semantics=("arbitrary", "arbitrary"), vmem_limit_bytes=VMEM_LIMIT),
        name="ffn",
    )(x, o_d, o_m, conv_prev, *weights)


def _rope_tables(pos):
    half = DIFF_HEAD_DIM // 2
    inv = ROPE_THETA ** (-jnp.arange(half, dtype=_F32) * (2.0 / DIFF_HEAD_DIM))
    ang = pos.astype(_F32)[:, None] * inv[None, :]
    cos, sin, zero = jnp.cos(ang), jnp.sin(ang), jnp.zeros_like(ang)
    cos2 = jnp.concatenate([cos, cos, cos, cos], axis=1)
    sin_lo = jnp.concatenate([-sin, zero, -sin, zero], axis=1)
    sin_hi = jnp.concatenate([zero, sin, zero, sin], axis=1)
    return cos2, sin_lo, sin_hi


def _tiles(T, n_keys):
    if T >= 512:
        return dict(tm=512, diff=(512, 512), mla=(256, 512))
    tk = -(-n_keys // LANES) * LANES
    return dict(tm=T, diff=(T, tk), mla=(T, tk))


def _layer(x, pos, past, lam_init, wl, g_final):
    (g_attn, w_in_p, lam_p, g_sub, g_qa, w_qn, w_qr, wkt, wv, g_kva, w_od, w_om, g_ffn, w_up,
     w_conv, b_conv, w_down) = wl
    B, T, _ = x.shape
    n_past = 0 if past is None else past[0].shape[1]
    n_keys, q_pos0 = n_past + T, n_past
    tl = _tiles(T, n_keys)
    tabs = _rope_tables(pos)
    qd, kd32, kd16, vd32, vd16, ckv32, kr32, kmla, vmla, qmla = _proj_call(
        x, tabs, g_attn, w_in_p, g_qa, w_qn, w_qr, wkt, g_kva, tm=tl["tm"], q_pos0=q_pos0)

    if past is None:
        kd_all, vd_all, kmla_all, vmla_all = kd16, vd16, kmla, vmla
        conv_prev = jnp.zeros((B, CONV_W - 1, D_FF), _F32)
    else:
        p_dk, p_dv, p_ckv, p_kr, conv_prev = past
        n_pad = tl["diff"][1] - n_keys
        lane = jnp.arange(LANES)[None, :]
        k_cols = jnp.where(((jnp.arange(n_past) // CHUNK)[:, None] > lane) & (lane < MASK_COLS),
                           NEG_INF, 0.0).astype(_BF16)
        k_cols = jnp.broadcast_to(k_cols, (B, n_past, LANES))
        pad_cols = jnp.where(lane < MASK_COLS, NEG_INF, 0.0).astype(_BF16)
        one = jnp.ones((B, n_past, LANES), _BF16)
        p_dk = p_dk.reshape(B, n_past, DIFF_WIDTH).astype(_BF16)
        p_dv = p_dv.reshape(B, n_past, DIFF_WIDTH).astype(_BF16)
        heads = lambda a: [a[:, :, h * LANES:(h + 1) * LANES] for h in range(DIFF_HEADS)]

        def cat(old, new, pad_row):
            pad = jnp.broadcast_to(pad_row, (B, n_pad, new.shape[-1]))
            return jnp.concatenate([jnp.concatenate(old, axis=-1), new, pad], axis=1)

        zero = lambda w: jnp.zeros((1, 1, w), _BF16)
        kd_all = cat([p for a in heads(p_dk) for p in (a, k_cols)], kd16,
                     jnp.tile(jnp.concatenate([zero(LANES)[0], pad_cols], axis=-1), DIFF_HEADS))
        vd_all = cat([p for a in heads(p_dv) for p in (a, one)], vd16, zero(DIFF_HEADS * V_ONES))
        kmla_all = cat([p_ckv.astype(_BF16), k_cols[:, :, :MASK_COLS], p_kr.astype(_BF16)], kmla,
                       jnp.concatenate([zero(LANES)[0], pad_cols[:, :MASK_COLS],
                                        zero(MLA_ROPE)[0]], -1))
        vmla_all = cat([p_ckv.astype(_BF16), one], vmla, zero(V_ONES))

    o_d = _diff_attn_call(qd, kd_all, vd_all, lam_p, tq=tl["diff"][0], tk=tl["diff"][1],
                          q_pos0=q_pos0, n_keys=n_keys, lam_init=lam_init)
    o_m = _mla_attn_call(qmla, kmla_all, vmla_all, tq=tl["mla"][0], tk=tl["mla"][1],
                         q_pos0=q_pos0, n_keys=n_keys)
    y, new_conv = _ffn_call(x, o_d, o_m, conv_prev, g_sub, wv, w_od, w_om, g_ffn, w_up, w_conv,
                            b_conv, w_down, g_final, tm=tl["tm"], lam_init=lam_init)
    state = (kd32.reshape(1, B, T, DIFF_HEADS, 2, DIFF_HEAD_DIM),
             vd32[None],
             ckv32[None], kr32[None], new_conv[None])
    return y, state


def kernel(x_prompt, x_sample, cache_diff_k, cache_diff_v, cache_mla_ckv, cache_mla_krope,
           state_conv, g_attn, w_in, lambda_q1, lambda_k1, lambda_q2, lambda_k2, g_diff_sub,
           g_q_lora, w_q_b, g_kv_lora, w_kv_b, w_out, g_ffn, w_up, w_conv, b_conv, w_down,
           g_final):
    assert g_attn.shape[0] == 1, "single-layer model"
    S = x_prompt.shape[1]
    T = x_sample.shape[1]
    P = cache_diff_k.shape[2]
    lam_init = 0.8 - 0.6 * math.exp(-0.3 * 0)

    w_in_p = jnp.concatenate([w_in[0], w_in[0][:, OFF_KR:IN_COLS]], axis=1).astype(_BF16)
    wq = w_q_b[0].reshape(MLA_Q_LORA, MLA_HEADS, MLA_NOPE + MLA_ROPE)
    w_qn = wq[:, :, :MLA_NOPE].reshape(MLA_Q_LORA, MLA_HEADS * MLA_NOPE).astype(_BF16)
    w_qr = wq[:, :, MLA_NOPE:].reshape(MLA_Q_LORA, MLA_HEADS * MLA_ROPE).astype(_BF16)
    wkv = w_kv_b[0].reshape(MLA_KV_LORA, MLA_HEADS, MLA_NOPE + MLA_V)
    wkt = jnp.transpose(wkv[:, :, :MLA_NOPE], (1, 2, 0)).astype(_BF16)
    wv = jnp.transpose(wkv[:, :, MLA_NOPE:], (1, 0, 2)).astype(_BF16)
    lam_p = jnp.concatenate([lambda_q1, lambda_k1, lambda_q2, lambda_k2], axis=0)
    lam_p = jnp.pad(lam_p, ((0, 16 - lam_p.shape[0]), (0, LANES - lam_p.shape[1])))
    wl = (g_attn, w_in_p, lam_p, g_diff_sub, g_q_lora, w_qn, w_qr, wkt, wv, g_kv_lora,
          w_out[0][:DIFF_WIDTH].astype(_BF16), w_out[0][DIFF_WIDTH:].astype(_BF16), g_ffn,
          w_up[0].astype(_BF16), w_conv[0], b_conv, w_down[0].astype(_BF16))
    g_fin = g_final[None]

    pos_p = jnp.arange(S, dtype=jnp.int32)
    pos_s = P + jnp.arange(T, dtype=jnp.int32)
    y_p, st_p = _layer(x_prompt, pos_p, None, lam_init, wl, g_fin)
    past = (cache_diff_k[0], cache_diff_v[0], cache_mla_ckv[0], cache_mla_krope[0], state_conv[0])
    y_s, st_s = _layer(x_sample, pos_s, past, lam_init, wl, g_fin)
    return (y_p, y_s) + st_p + st_s
```
